```python
import math
import jax, jax.numpy as jnp
from jax import lax
import numpy as np

D_MODEL = 2048
BATCH = 8
SEQ = 2048
DEPTH = 1

SSM_WIDTH = D_MODEL // 2
SSM_GROUP = 16
SSM_GROUPS = SSM_WIDTH // SSM_GROUP
SSM_STATE = 64
DT_MIN = 1e-3
DT_MAX = 1e-1
CONV_WIDTH = D_MODEL // 2
CONV_K = 3
D_FF = 5504
EPS = 1e-6
IN_COLS = SSM_WIDTH + 3 * CONV_WIDTH + 2 * D_MODEL

kernel_name = "hybrid_s5_shortconv_macaron_block"


def _rmsnorm(x, g):
    xf = x.astype(jnp.float32)
    y = xf * lax.rsqrt(jnp.mean(xf * xf, axis=-1, keepdims=True) + EPS)
    return (y * g.astype(jnp.float32)).astype(x.dtype)


def _swiglu(x, w_gate, w_up, w_down):
    return (jax.nn.silu(x @ w_gate) * (x @ w_up)) @ w_down


def _s5_discretize(lam_re, lam_im, log_dt, b_re, b_im):
    lam_re = jnp.minimum(lam_re, -1e-4)
    dt = jnp.exp(log_dt)[:, None]
    mag = jnp.exp(lam_re * dt)
    a_re = mag * jnp.cos(lam_im * dt)
    a_im = mag * jnp.sin(lam_im * dt)
    den = lam_re * lam_re + lam_im * lam_im
    p = a_re - 1.0
    f_re = (p * lam_re + a_im * lam_im) / den
    f_im = (a_im * lam_re - p * lam_im) / den
    f_re = f_re[:, :, None]
    f_im = f_im[:, :, None]
    bb_re = f_re * b_re - f_im * b_im
    bb_im = f_re * b_im + f_im * b_re
    return a_re, a_im, bb_re, bb_im


def _ssm_combine(left, right):
    a1r, a1i, b1r, b1i = left
    a2r, a2i, b2r, b2i = right
    ar = a2r * a1r - a2i * a1i
    ai = a2r * a1i + a2i * a1r
    br = a2r * b1r - a2i * b1i + b2r
    bi = a2r * b1i + a2i * b1r + b2i
    return ar, ai, br, bi


def _s5_branch(v, lam_re, lam_im, log_dt, b_re, b_im, c_re, c_im, d_skip, w_glu, b_glu):
    bsz, seq, _ = v.shape
    vf = v.astype(jnp.float32).reshape(bsz, seq, SSM_GROUPS, SSM_GROUP)
    a_re, a_im, bb_re, bb_im = _s5_discretize(lam_re, lam_im, log_dt, b_re, b_im)
    bu_re = jnp.einsum('bsgc,gnc->bsgn', vf, bb_re)
    bu_im = jnp.einsum('bsgc,gnc->bsgn', vf, bb_im)
    shp = (1, seq, SSM_GROUPS, SSM_STATE)
    ar = jnp.broadcast_to(a_re[None, None], shp)
    ai = jnp.broadcast_to(a_im[None, None], shp)
    _, _, s_re, s_im = lax.associative_scan(_ssm_combine, (ar, ai, bu_re, bu_im), axis=1)
    y = (jnp.einsum('bsgn,gcn->bsgc', s_re, c_re)
         - jnp.einsum('bsgn,gcn->bsgc', s_im, c_im))
    y = y.reshape(bsz, seq, SSM_WIDTH) + d_skip * vf.reshape(bsz, seq, SSM_WIDTH)
    y = jax.nn.gelu(y)
    y = y * jax.nn.sigmoid(y @ w_glu + b_glu)
    return y.astype(v.dtype)


def _short_conv_branch(b_gate, c_gate, val, conv_w, conv_b):
    seq = val.shape[1]
    z = c_gate * val
    zp = jnp.pad(z, ((0, 0), (CONV_K - 1, 0), (0, 0)))
    conv = conv_b + sum(conv_w[k] * zp[:, k:k + seq] for k in range(CONV_K))
    return b_gate * conv


def setup_inputs(seed: int = 0) -> dict:
    key = jax.random.key(seed)
    ks = jax.random.split(key, 32)
    f32 = jnp.float32
    D, F, W, CW, G, N, C = D_MODEL, D_FF, SSM_WIDTH, CONV_WIDTH, SSM_GROUPS, SSM_STATE, SSM_GROUP

    def nrm(k, shape, scale):
        return jax.random.normal(k, shape, f32) * scale

    def gain(k, n):
        return 1.0 + 0.02 * jax.random.normal(k, (n,), f32)

    return {
        "x": jax.random.normal(ks[0], (BATCH, SEQ, D), f32),
        "ffn1_norm": gain(ks[1], D),
        "ffn1_w_gate": nrm(ks[2], (D, F), D ** -0.5),
        "ffn1_w_up": nrm(ks[3], (D, F), D ** -0.5),
        "ffn1_w_down": nrm(ks[4], (F, D), F ** -0.5),
        "mix_norm": gain(ks[5], D),
        "w_in": nrm(ks[6], (D, IN_COLS), D ** -0.5),
        "ssm_lambda_re": -0.5 + 0.01 * jax.random.normal(ks[7], (G, N), f32),
        "ssm_lambda_im": math.pi * jnp.broadcast_to(jnp.arange(N, dtype=f32), (G, N))
                         + 0.01 * jax.random.normal(ks[8], (G, N), f32),
        "ssm_log_dt": jax.random.uniform(ks[9], (G,), f32, math.log(DT_MIN), math.log(DT_MAX)),
        "ssm_b_re": nrm(ks[10], (G, N, C), (2 * C) ** -0.5),
        "ssm_b_im": nrm(ks[11], (G, N, C), (2 * C) ** -0.5),
        "ssm_c_re": nrm(ks[12], (G, C, N), N ** -0.5),
        "ssm_c_im": nrm(ks[13], (G, C, N), N ** -0.5),
        "ssm_d": nrm(ks[14], (W,), 1.0),
        "ssm_w_glu": nrm(ks[15], (W, W), W ** -0.5),
        "ssm_b_glu": nrm(ks[16], (W,), 0.01),
        "ssm_w_out": nrm(ks[17], (W, D), W ** -0.5),
        "conv_w": nrm(ks[18], (CONV_K, CW), CONV_K ** -0.5),
        "conv_b": nrm(ks[19], (CW,), 0.01),
        "conv_w_out": nrm(ks[20], (CW, D), CW ** -0.5),
        "w_o": nrm(ks[21], (D, D), D ** -0.5),
        "ffn2_norm": gain(ks[22], D),
        "ffn2_w_gate": nrm(ks[23], (D, F), D ** -0.5),
        "ffn2_w_up": nrm(ks[24], (D, F), D ** -0.5),
        "ffn2_w_down": nrm(ks[25], (F, D), F ** -0.5),
        "final_norm": gain(ks[26], D),
    }


def reference(x, ffn1_norm, ffn1_w_gate, ffn1_w_up, ffn1_w_down, mix_norm, w_in,
              ssm_lambda_re, ssm_lambda_im, ssm_log_dt, ssm_b_re, ssm_b_im, ssm_c_re, ssm_c_im,
              ssm_d, ssm_w_glu, ssm_b_glu, ssm_w_out, conv_w, conv_b, conv_w_out, w_o,
              ffn2_norm, ffn2_w_gate, ffn2_w_up, ffn2_w_down, final_norm):
    h = x
    for _ in range(DEPTH):
        h = h + 0.5 * _swiglu(_rmsnorm(h, ffn1_norm), ffn1_w_gate, ffn1_w_up, ffn1_w_down)
        u = _rmsnorm(h, mix_norm)
        proj = u @ w_in
        splits = [SSM_WIDTH, SSM_WIDTH + CONV_WIDTH, SSM_WIDTH + 2 * CONV_WIDTH,
                  SSM_WIDTH + 3 * CONV_WIDTH, SSM_WIDTH + 3 * CONV_WIDTH + D_MODEL]
        v_ssm, b_gate, c_gate, val, ga_pre, gb_pre = jnp.split(proj, splits, axis=-1)
        y_a = _s5_branch(v_ssm, ssm_lambda_re, ssm_lambda_im, ssm_log_dt, ssm_b_re, ssm_b_im,
                         ssm_c_re, ssm_c_im, ssm_d, ssm_w_glu, ssm_b_glu)
        y_b = _short_conv_branch(b_gate, c_gate, val, conv_w, conv_b)
        z_a = y_a @ ssm_w_out
        z_b = y_b @ conv_w_out
        merged = jax.nn.sigmoid(ga_pre) * z_a + jax.nn.sigmoid(gb_pre) * z_b
        h = h + merged @ w_o
        h = h + 0.5 * _swiglu(_rmsnorm(h, ffn2_norm), ffn2_w_gate, ffn2_w_up, ffn2_w_down)
    return _rmsnorm(h, final_norm)
```

```python
import functools

import jax
import jax.numpy as jnp
from jax import lax
from jax.experimental import pallas as pl
from jax.experimental.pallas import tpu as pltpu

D_MODEL = 2048
D_FF = 5504
SSM_WIDTH = 1024
SSM_GROUP = 16
SSM_GROUPS = 64
SSM_STATE = 64
CONV_WIDTH = 1024
CONV_K = 3
EPS = 1e-6

LANES = 128
MXU_DIM = 256
D_FF_PAD = 5632
CHUNK = 16
PLANES = SSM_WIDTH // LANES
GROUPS_PER_PLANE = LANES // SSM_GROUP
PLANE_STATE = 2 * GROUPS_PER_PLANE * SSM_STATE
SCAN_STEPS = 7
VMEM_LIMIT = 56 * 1024 * 1024

BF16 = jnp.bfloat16
F32 = jnp.float32


def _rms(x, g):
    return x * lax.rsqrt(jnp.mean(x * x, axis=-1, keepdims=True) + EPS) * g


def _dot(a, b):
    return jnp.dot(a, b, preferred_element_type=F32)


def _ffn_kernel(x_ref, gin_ref, wg_ref, wu_ref, wd_ref, gout_ref, *refs, emit_hidden):
    if emit_hidden:
        h_ref, u_ref, xn_ref, acc_ref = refs
    else:
        u_ref, xn_ref, acc_ref = refs
    f = pl.program_id(1)

    @pl.when(f == 0)
    def _():
        xn_ref[...] = _rms(x_ref[...], gin_ref[...]).astype(BF16)
        acc_ref[...] = jnp.zeros_like(acc_ref)

    xn = xn_ref[...]
    gate = _dot(xn, wg_ref[...])
    up = _dot(xn, wu_ref[...])
    act = (gate * jax.nn.sigmoid(gate) * up).astype(BF16)
    acc_ref[...] += _dot(act, wd_ref[...])

    @pl.when(f == pl.num_programs(1) - 1)
    def _():
        h = x_ref[...] + 0.5 * acc_ref[...]
        if emit_hidden:
            h_ref[...] = h
        u_ref[...] = _rms(h, gout_ref[...]).astype(u_ref.dtype)


def _ffn(x, g_in, wg, wu, wd, g_out, *, emit_hidden, tm=512, tf=512):
    t, d = x.shape
    fp = wg.shape[1]
    grid = (t // tm, fp // tf)
    row = pl.BlockSpec((tm, d), lambda i, f: (i, 0))
    vec = pl.BlockSpec((1, d), lambda i, f: (0, 0))
    if emit_hidden:
        out_shape = (jax.ShapeDtypeStruct((t, d), F32), jax.ShapeDtypeStruct((t, d), BF16))
        out_specs = (row, row)
    else:
        out_shape = jax.ShapeDtypeStruct((t, d), F32)
        out_specs = row
    return pl.pallas_call(
        functools.partial(_ffn_kernel, emit_hidden=emit_hidden),
        grid=grid,
        in_specs=[
            row, vec,
            pl.BlockSpec((d, tf), lambda i, f: (0, f)),
            pl.BlockSpec((d, tf), lambda i, f: (0, f)),
            pl.BlockSpec((tf, d), lambda i, f: (f, 0)),
            vec,
        ],
        out_specs=out_specs,
        out_shape=out_shape,
        scratch_shapes=[pltpu.VMEM((tm, d), BF16), pltpu.VMEM((tm, d), F32)],
        compiler_params=pltpu.CompilerParams(
            dimension_semantics=("parallel", "arbitrary"),
            vmem_limit_bytes=VMEM_LIMIT),
        name="ffn_hidden" if emit_hidden else "ffn_final",
    )(x, g_in, wg, wu, wd, g_out)


def _ssm_in_kernel(u_ref, w_ref, o_ref):
    res = _dot(u_ref[...], w_ref[...])
    for j in range(PLANES):
        o_ref[j] = res[:, j * LANES:(j + 1) * LANES].astype(o_ref.dtype)


def _ssm_in(u, w, *, tm=1024):
    t, d = u.shape
    return pl.pallas_call(
        _ssm_in_kernel,
        grid=(t // tm,),
        in_specs=[pl.BlockSpec((tm, d), lambda i: (i, 0)),
                  pl.BlockSpec((d, SSM_WIDTH), lambda i: (0, 0))],
        out_specs=pl.BlockSpec((PLANES, tm, LANES), lambda i: (0, i, 0)),
        out_shape=jax.ShapeDtypeStruct((PLANES, t, LANES), BF16),
        compiler_params=pltpu.CompilerParams(
            dimension_semantics=("parallel",), vmem_limit_bytes=VMEM_LIMIT),
        name="ssm_in_proj",
    )(u, w)


def _conv_kernel(u_ref, wb_ref, wc_ref, wv_ref, cw_ref, cb_ref, o_ref, tail_ref, *, cblk):
    s = pl.program_id(1)

    @pl.when(s == 0)
    def _():
        tail_ref[...] = jnp.zeros_like(tail_ref)

    u = u_ref[...]
    ts = u.shape[0]
    row = lax.broadcasted_iota(jnp.int32, (ts, 1), 0)
    for c in range(CONV_WIDTH // cblk):
        cs = slice(c * cblk, (c + 1) * cblk)
        z = _dot(u, wc_ref[:, cs]) * _dot(u, wv_ref[:, cs])
        prev1 = tail_ref[7:8, cs]
        prev2 = tail_ref[6:7, cs]
        z1 = jnp.where(row == 0, prev1, pltpu.roll(z, 1, axis=0))
        z2 = jnp.where(row == 0, prev2, jnp.where(row == 1, prev1, pltpu.roll(z, 2, axis=0)))
        conv = cb_ref[:, cs] + cw_ref[0:1, cs] * z2 + cw_ref[1:2, cs] * z1 + cw_ref[2:3, cs] * z
        o_ref[:, cs] = (_dot(u, wb_ref[:, cs]) * conv).astype(o_ref.dtype)
        tail_ref[:, cs] = z[ts - 8:, :]


def _conv_branch(u, wb, wc, wv, conv_w, conv_b, *, batch, seq, ts=512, cblk=256):
    t, d = u.shape
    nseq = seq // ts
    wspec = pl.BlockSpec((d, CONV_WIDTH), lambda b, s: (0, 0))
    return pl.pallas_call(
        functools.partial(_conv_kernel, cblk=cblk),
        grid=(batch, nseq),
        in_specs=[pl.BlockSpec((ts, d), lambda b, s: (b * nseq + s, 0)),
                  wspec, wspec, wspec,
                  pl.BlockSpec((CONV_K, CONV_WIDTH), lambda b, s: (0, 0)),
                  pl.BlockSpec((1, CONV_WIDTH), lambda b, s: (0, 0))],
        out_specs=pl.BlockSpec((ts, CONV_WIDTH), lambda b, s: (b * nseq + s, 0)),
        out_shape=jax.ShapeDtypeStruct((t, CONV_WIDTH), BF16),
        scratch_shapes=[pltpu.VMEM((8, CONV_WIDTH), F32)],
        compiler_params=pltpu.CompilerParams(
            dimension_semantics=("parallel", "arbitrary"), vmem_limit_bytes=VMEM_LIMIT),
        name="conv_branch",
    )(u, wb, wc, wv, conv_w, conv_b)


def _gate_kernel(u_ref, w_ref, o_ref):
    o_ref[...] = jax.nn.sigmoid(_dot(u_ref[...], w_ref[...])).astype(o_ref.dtype)


def _gates(u, w, *, tm=1024, tn=1024):
    t, d = u.shape
    n = w.shape[1]
    return pl.pallas_call(
        _gate_kernel,
        grid=(t // tm, n // tn),
        in_specs=[pl.BlockSpec((tm, d), lambda i, j: (i, 0)),
                  pl.BlockSpec((d, tn), lambda i, j: (0, j))],
        out_specs=pl.BlockSpec((tm, tn), lambda i, j: (i, j)),
        out_shape=jax.ShapeDtypeStruct((t, n), BF16),
        compiler_params=pltpu.CompilerParams(
            dimension_semantics=("parallel", "parallel"), vmem_limit_bytes=VMEM_LIMIT),
        name="merge_gates",
    )(u, w)


def _ssm_kernel(x_ref, toep_ref, win_ref, wout_ref, apow_ref, d_ref, o_ref, *, chunks):
    x = x_ref[...]
    rows = x.shape[0]
    half = PLANE_STATE // 2
    y = _dot(x, toep_ref[...])
    s_in = _dot(x, win_ref[...])
    sr = s_in[:, :half]
    si = s_in[:, half:]
    cidx = lax.broadcasted_iota(jnp.int32, (rows, 1), 0) % chunks
    for k in range(SCAN_STEPS):
        sh = 1 << k
        ar = apow_ref[k:k + 1, :half]
        ai = apow_ref[k:k + 1, half:]
        keep = cidx >= sh
        pr = jnp.where(keep, pltpu.roll(sr, sh, axis=0), 0.0)
        pi = jnp.where(keep, pltpu.roll(si, sh, axis=0), 0.0)
        sr, si = sr + (ar * pr - ai * pi), si + (ar * pi + ai * pr)
    first = cidx >= 1
    pr = jnp.where(first, pltpu.roll(sr, 1, axis=0), 0.0).astype(BF16)
    pi = jnp.where(first, pltpu.roll(si, 1, axis=0), 0.0).astype(BF16)
    y = y + _dot(pr, wout_ref[:half, :]) + _dot(pi, wout_ref[half:, :])
    y = y + d_ref[...] * x.astype(F32)
    o_ref[...] = jax.nn.gelu(y).astype(o_ref.dtype)


def _ssm(xc, toep, w_in, w_out, apow, d_tiled, *, chunks, rows=512):
    planes, nrows, width = xc.shape
    nh = nrows // rows
    return pl.pallas_call(
        functools.partial(_ssm_kernel, chunks=chunks),
        grid=(planes, nh),
        in_specs=[
            pl.BlockSpec((None, rows, width), lambda j, h: (j, h, 0)),
            pl.BlockSpec((None, width, width), lambda j, h: (j, 0, 0)),
            pl.BlockSpec((None, width, PLANE_STATE), lambda j, h: (j, 0, 0)),
            pl.BlockSpec((None, PLANE_STATE, width), lambda j, h: (j, 0, 0)),
            pl.BlockSpec((None, 8, PLANE_STATE), lambda j, h: (j, 0, 0)),
            pl.BlockSpec((None, 1, width), lambda j, h: (j, 0, 0)),
        ],
        out_specs=pl.BlockSpec((None, rows, width), lambda j, h: (j, h, 0)),
        out_shape=jax.ShapeDtypeStruct((planes, nrows, width), BF16),
        compiler_params=pltpu.CompilerParams(
            dimension_semantics=("parallel", "parallel"), vmem_limit_bytes=VMEM_LIMIT),
        name="ssm_scan",
    )(xc, toep, w_in, w_out, apow, d_tiled)


def _ssm_operators(lam_re, lam_im, log_dt, b_re, b_im, c_re, c_im, d_skip):
    hi = lax.Precision.HIGHEST
    lam_re = jnp.minimum(lam_re, -1e-4)
    dt = jnp.exp(log_dt)[:, None]
    mag = jnp.exp(lam_re * dt)
    a_re = mag * jnp.cos(lam_im * dt)
    a_im = mag * jnp.sin(lam_im * dt)
    den = lam_re * lam_re + lam_im * lam_im
    p = a_re - 1.0
    f_re = ((p * lam_re + a_im * lam_im) / den)[:, :, None]
    f_im = ((a_im * lam_re - p * lam_im) / den)[:, :, None]
    bb_re = f_re * b_re - f_im * b_im
    bb_im = f_re * b_im + f_im * b_re

    def apow(k):
        k = jnp.asarray(k, F32).reshape(-1, 1, 1)
        m = jnp.exp(k * (lam_re * dt))
        ph = k * (lam_im * dt)
        return m * jnp.cos(ph), m * jnp.sin(ph)

    G, N, C, P, GP, L = SSM_GROUPS, SSM_STATE, SSM_GROUP, PLANES, GROUPS_PER_PLANE, CHUNK
    eye = jnp.eye(GP, dtype=F32)

    pr, pi = apow(jnp.arange(L + 1))
    abr = pr[:L, :, :, None] * bb_re - pi[:L, :, :, None] * bb_im
    abi = pr[:L, :, :, None] * bb_im + pi[:L, :, :, None] * bb_re
    kern = (jnp.einsum('gon,kgnc->kgoc', c_re, abr, precision=hi)
            - jnp.einsum('gon,kgnc->kgoc', c_im, abi, precision=hi))
    lag = jnp.arange(L)[None, :] - jnp.arange(L)[:, None]
    kt = jnp.where((lag >= 0)[:, :, None, None, None], kern[jnp.clip(lag, 0, L - 1)], 0.0)
    kt = kt.reshape(L, L, P, GP, C, C)
    toep = jnp.einsum('pq,abjpoc->japcbqo', eye, kt).reshape(P, L * LANES, L * LANES)

    sin_r = abr[::-1].reshape(L, P, GP, N, C)
    sin_i = abi[::-1].reshape(L, P, GP, N, C)
    w_in = jnp.stack([jnp.einsum('pq,ajpnc->japcqn', eye, sin_r),
                      jnp.einsum('pq,ajpnc->japcqn', eye, sin_i)], axis=4)
    w_in = w_in.reshape(P, L * LANES, PLANE_STATE)

    car = c_re[None] * pr[1:, :, None, :] - c_im[None] * pi[1:, :, None, :]
    cai = c_re[None] * pi[1:, :, None, :] + c_im[None] * pr[1:, :, None, :]
    car = car.reshape(L, P, GP, C, N)
    cai = cai.reshape(L, P, GP, C, N)
    w_out = jnp.stack([jnp.einsum('pq,bjpon->jpnbqo', eye, car),
                       jnp.einsum('pq,bjpon->jpnbqo', eye, -cai)], axis=1)
    w_out = w_out.reshape(P, PLANE_STATE, L * LANES)

    sr, si = apow(L * (2 ** jnp.arange(8)))
    scan = jnp.concatenate([sr.reshape(8, P, GP * N), si.reshape(8, P, GP * N)], axis=2)
    scan = jnp.transpose(scan, (1, 0, 2))

    d_tiled = jnp.tile(d_skip.reshape(P, 1, LANES), (1, 1, L))
    return toep.astype(BF16), w_in.astype(BF16), w_out.astype(BF16), scan, d_tiled


def _mix_out_kernel(g_ref, yb_ref, gate_ref, h_ref, wglu_ref, bglu_ref, wa_ref, wb_ref, wo_ref, o_ref):
    g = jnp.concatenate([g_ref[j] for j in range(PLANES)], axis=1)
    glu = _dot(g, wglu_ref[...]) + bglu_ref[...]
    y_a = (g.astype(F32) * jax.nn.sigmoid(glu)).astype(BF16)
    z_a = _dot(y_a, wa_ref[...])
    z_b = _dot(yb_ref[...], wb_ref[...])
    merged = (gate_ref[:, :D_MODEL].astype(F32) * z_a
              + gate_ref[:, D_MODEL:].astype(F32) * z_b).astype(BF16)
    o_ref[...] = h_ref[...] + _dot(merged, wo_ref[...])


def _mix_out(g_planes, y_b, gates, h, w_glu, b_glu, w_a, w_b, w_o, *, tm=512):
    t, d = h.shape

    def const(shape):
        return pl.BlockSpec(shape, lambda i: (0,) * len(shape), pipeline_mode=pl.Buffered(1))

    return pl.pallas_call(
        _mix_out_kernel,
        grid=(t // tm,),
        in_specs=[
            pl.BlockSpec((PLANES, tm, LANES), lambda i: (0, i, 0)),
            pl.BlockSpec((tm, CONV_WIDTH), lambda i: (i, 0)),
            pl.BlockSpec((tm, 2 * d), lambda i: (i, 0)),
            pl.BlockSpec((tm, d), lambda i: (i, 0)),
            const((SSM_WIDTH, SSM_WIDTH)), const((1, SSM_WIDTH)),
            const((SSM_WIDTH, d)), const((CONV_WIDTH, d)), const((d, d)),
        ],
        out_specs=pl.BlockSpec((tm, d), lambda i: (i, 0)),
        out_shape=jax.ShapeDtypeStruct((t, d), F32),
        compiler_params=pltpu.CompilerParams(
            dimension_semantics=("parallel",), vmem_limit_bytes=VMEM_LIMIT),
        name="mix_out",
    )(g_planes, y_b, gates, h, w_glu, b_glu, w_a, w_b, w_o)


def kernel(x, ffn1_norm, ffn1_w_gate, ffn1_w_up, ffn1_w_down, mix_norm, w_in, ssm_lambda_re, ssm_lambda_im, ssm_log_dt, ssm_b_re, ssm_b_im, ssm_c_re, ssm_c_im, ssm_d, ssm_w_glu, ssm_b_glu, ssm_w_out, conv_w, conv_b, conv_w_out, w_o, ffn2_norm, ffn2_w_gate, ffn2_w_up, ffn2_w_down, final_norm):
    batch, seq, d = x.shape
    t = batch * seq
    chunks = seq // CHUNK
    pad_f = D_FF_PAD - D_FF

    def ffn_weights(wg, wu, wd):
        return (jnp.pad(wg.astype(BF16), ((0, 0), (0, pad_f))),
                jnp.pad(wu.astype(BF16), ((0, 0), (0, pad_f))),
                jnp.pad(wd.astype(BF16), ((0, pad_f), (0, 0))))

    vec = lambda g: g.reshape(1, -1).astype(F32)
    x2 = x.reshape(t, d)

    h1, u = _ffn(x2, vec(ffn1_norm), *ffn_weights(ffn1_w_gate, ffn1_w_up, ffn1_w_down),
                 vec(mix_norm), emit_hidden=True)

    w_in_b = w_in.astype(BF16)
    c0 = SSM_WIDTH
    v_planes = _ssm_in(u, w_in_b[:, :c0])
    y_b = _conv_branch(u, w_in_b[:, c0:c0 + CONV_WIDTH],
                       w_in_b[:, c0 + CONV_WIDTH:c0 + 2 * CONV_WIDTH],
                       w_in_b[:, c0 + 2 * CONV_WIDTH:c0 + 3 * CONV_WIDTH],
                       conv_w.astype(F32), vec(conv_b), batch=batch, seq=seq)
    gates = _gates(u, w_in_b[:, c0 + 3 * CONV_WIDTH:])

    toep, s_in, s_out, scan, d_tiled = _ssm_operators(
        ssm_lambda_re, ssm_lambda_im, ssm_log_dt, ssm_b_re, ssm_b_im, ssm_c_re, ssm_c_im, ssm_d)
    xc = v_planes.reshape(PLANES, t // CHUNK, CHUNK * LANES)
    g_planes = _ssm(xc, toep, s_in, s_out, scan, d_tiled, chunks=chunks)
    g_planes = g_planes.reshape(PLANES, t, LANES)

    h2 = _mix_out(g_planes, y_b, gates, h1, ssm_w_glu.astype(BF16), vec(ssm_b_glu),
                  ssm_w_out.astype(BF16), conv_w_out.astype(BF16), w_o.astype(BF16))

    out = _ffn(h2, vec(ffn2_norm), *ffn_weights(ffn2_w_gate, ffn2_w_up, ffn2_w_down),
               vec(final_norm), emit_hidden=False)
    return out.reshape(batch, seq, d)
```

```python
import functools

import jax
import jax.numpy as jnp
from jax import lax
from jax.experimental import pallas as pl
from jax.experimental.pallas import tpu as pltpu

D_MODEL = 2048
SSM_WIDTH = 1024
SSM_GROUP = 16
SSM_GROUPS = 64
SSM_STATE = 64
CONV_WIDTH = 1024
CONV_K = 3
EPS = 1e-6

LANES = 128
CHUNK = 16
PLANES = SSM_WIDTH // LANES
GROUPS_PER_PLANE = LANES // SSM_GROUP
HALF_STATE = GROUPS_PER_PLANE * SSM_STATE
PLANE_STATE = 2 * HALF_STATE
SCAN_STEPS = 7
SCAN_ROW0 = CHUNK + 1
APOW_ROWS = SCAN_ROW0 + SCAN_STEPS
ACOL_IM = LANES // 2
VMEM_LIMIT = 56 * 1024 * 1024

BF16 = jnp.bfloat16
F32 = jnp.float32
HIGHEST = lax.Precision.HIGHEST


def _rms(x, g):
    return x * lax.rsqrt(jnp.mean(x * x, axis=-1, keepdims=True) + EPS) * g


def _dot(a, b):
    return jnp.dot(a, b, preferred_element_type=F32)


def _ffn_kernel(x_ref, gin_ref, wg_ref, wu_ref, wd_ref, gout_ref, *refs, emit_hidden, d_ff):
    if emit_hidden:
        h_ref, u_ref, xn_ref, acc_ref = refs
    else:
        u_ref, xn_ref, acc_ref = refs
    f = pl.program_id(1)
    tf = wd_ref.shape[0]

    @pl.when(f == 0)
    def _():
        xn_ref[...] = _rms(x_ref[...], gin_ref[...]).astype(BF16)
        acc_ref[...] = jnp.zeros_like(acc_ref)

    xn = xn_ref[...]
    gate = _dot(xn, wg_ref[...])
    up = _dot(xn, wu_ref[...])
    valid = d_ff - f * tf
    col = lax.broadcasted_iota(jnp.int32, (1, tf), 1)
    act = jnp.where(col < valid, gate * jax.nn.sigmoid(gate) * up, 0.0).astype(BF16)
    wrow = lax.broadcasted_iota(jnp.int32, (tf, 1), 0)
    wd = jnp.where(wrow < valid, wd_ref[...], jnp.zeros((), BF16))
    acc_ref[...] += _dot(act, wd)

    @pl.when(f == pl.num_programs(1) - 1)
    def _():
        h = x_ref[...] + 0.5 * acc_ref[...]
        if emit_hidden:
            h_ref[...] = h
        u_ref[...] = _rms(h, gout_ref[...]).astype(u_ref.dtype)


def _ffn(x, g_in, wg, wu, wd, g_out, *, emit_hidden, tm=512, tf=512):
    t, d = x.shape
    d_ff = wg.shape[1]
    grid = (t // tm, pl.cdiv(d_ff, tf))
    row = pl.BlockSpec((tm, d), lambda i, f: (i, 0))
    vec = pl.BlockSpec((1, d), lambda i, f: (0, 0))
    if emit_hidden:
        out_shape = (jax.ShapeDtypeStruct((t, d), F32), jax.ShapeDtypeStruct((t, d), BF16))
        out_specs = (row, row)
    else:
        out_shape = jax.ShapeDtypeStruct((t, d), F32)
        out_specs = row
    return pl.pallas_call(
        functools.partial(_ffn_kernel, emit_hidden=emit_hidden, d_ff=d_ff),
        grid=grid,
        in_specs=[
            row, vec,
            pl.BlockSpec((d, tf), lambda i, f: (0, f)),
            pl.BlockSpec((d, tf), lambda i, f: (0, f)),
            pl.BlockSpec((tf, d), lambda i, f: (f, 0)),
            vec,
        ],
        out_specs=out_specs,
        out_shape=out_shape,
        scratch_shapes=[pltpu.VMEM((tm, d), BF16), pltpu.VMEM((tm, d), F32)],
        compiler_params=pltpu.CompilerParams(
            dimension_semantics=("parallel", "arbitrary"),
            vmem_limit_bytes=VMEM_LIMIT),
        name="ffn_hidden" if emit_hidden else "ffn_final",
    )(x, g_in, wg, wu, wd, g_out)


def _ssm_in_kernel(u_ref, w_ref, o_ref):
    res = _dot(u_ref[...], w_ref[...])
    for j in range(PLANES):
        o_ref[j] = res[:, j * LANES:(j + 1) * LANES].astype(o_ref.dtype)


def _ssm_in(u, w, *, tm=1024):
    t, d = u.shape
    return pl.pallas_call(
        _ssm_in_kernel,
        grid=(t // tm,),
        in_specs=[pl.BlockSpec((tm, d), lambda i: (i, 0)),
                  pl.BlockSpec((d, SSM_WIDTH), lambda i: (0, 0))],
        out_specs=pl.BlockSpec((PLANES, tm, LANES), lambda i: (0, i, 0)),
        out_shape=jax.ShapeDtypeStruct((PLANES, t, LANES), BF16),
        compiler_params=pltpu.CompilerParams(
            dimension_semantics=("parallel",), vmem_limit_bytes=VMEM_LIMIT),
        name="ssm_in_proj",
    )(u, w)


def _conv_kernel(u_ref, wb_ref, wc_ref, wv_ref, cw_ref, cb_ref, o_ref, tail_ref, *, cblk):
    s = pl.program_id(1)

    @pl.when(s == 0)
    def _():
        tail_ref[...] = jnp.zeros_like(tail_ref)

    u = u_ref[...]
    ts = u.shape[0]
    row = lax.broadcasted_iota(jnp.int32, (ts, 1), 0)
    for c in range(CONV_WIDTH // cblk):
        cs = slice(c * cblk, (c + 1) * cblk)
        z = _dot(u, wc_ref[:, cs]) * _dot(u, wv_ref[:, cs])
        prev1 = tail_ref[7:8, cs]
        prev2 = tail_ref[6:7, cs]
        z1 = jnp.where(row == 0, prev1, pltpu.roll(z, 1, axis=0))
        z2 = jnp.where(row == 0, prev2, jnp.where(row == 1, prev1, pltpu.roll(z, 2, axis=0)))
        conv = cb_ref[:, cs] + cw_ref[0:1, cs] * z2 + cw_ref[1:2, cs] * z1 + cw_ref[2:3, cs] * z
        o_ref[:, cs] = (_dot(u, wb_ref[:, cs]) * conv).astype(o_ref.dtype)
        tail_ref[:, cs] = z[ts - 8:, :]


def _conv_branch(u, wb, wc, wv, conv_w, conv_b, *, batch, seq, ts=512, cblk=256):
    t, d = u.shape
    nseq = seq // ts
    wspec = pl.BlockSpec((d, CONV_WIDTH), lambda b, s: (0, 0))
    return pl.pallas_call(
        functools.partial(_conv_kernel, cblk=cblk),
        grid=(batch, nseq),
        in_specs=[pl.BlockSpec((ts, d), lambda b, s: (b * nseq + s, 0)),
                  wspec, wspec, wspec,
                  pl.BlockSpec((CONV_K, CONV_WIDTH), lambda b, s: (0, 0)),
                  pl.BlockSpec((1, CONV_WIDTH), lambda b, s: (0, 0))],
        out_specs=pl.BlockSpec((ts, CONV_WIDTH), lambda b, s: (b * nseq + s, 0)),
        out_shape=jax.ShapeDtypeStruct((t, CONV_WIDTH), BF16),
        scratch_shapes=[pltpu.VMEM((8, CONV_WIDTH), F32)],
        compiler_params=pltpu.CompilerParams(
            dimension_semantics=("parallel", "arbitrary"), vmem_limit_bytes=VMEM_LIMIT),
        name="conv_branch",
    )(u, wb, wc, wv, conv_w, conv_b)


def _gate_kernel(u_ref, w_ref, o_ref):
    o_ref[...] = jax.nn.sigmoid(_dot(u_ref[...], w_ref[...])).astype(o_ref.dtype)


def _gates(u, w, *, tm=1024, tn=1024):
    t, d = u.shape
    n = w.shape[1]
    return pl.pallas_call(
        _gate_kernel,
        grid=(t // tm, n // tn),
        in_specs=[pl.BlockSpec((tm, d), lambda i, j: (i, 0)),
                  pl.BlockSpec((d, tn), lambda i, j: (0, j))],
        out_specs=pl.BlockSpec((tm, tn), lambda i, j: (i, j)),
        out_shape=jax.ShapeDtypeStruct((t, n), BF16),
        compiler_params=pltpu.CompilerParams(
            dimension_semantics=("parallel", "parallel"), vmem_limit_bytes=VMEM_LIMIT),
        name="merge_gates",
    )(u, w)


def _ssm_kernel(x_ref, btr_ref, bti_ref, ctr_ref, cti_ref, arow_ref, acol_ref, d_ref, o_ref,
                toep_ref, win_ref, wout_ref, *, chunks):
    half = HALF_STATE

    @pl.when(pl.program_id(1) == 0)
    def _build_operators():
        btr = btr_ref[...]
        bti = bti_ref[...]
        ctr = ctr_ref[...]
        cti = cti_ref[...]
        zero = jnp.zeros((LANES, LANES), BF16)
        for k in range(CHUNK):
            ar = arow_ref[k:k + 1, :half]
            ai = arow_ref[k:k + 1, half:]
            wr = btr * ar - bti * ai
            wi = btr * ai + bti * ar
            rows = slice((CHUNK - 1 - k) * LANES, (CHUNK - k) * LANES)
            win_ref[rows, :half] = wr.astype(BF16)
            win_ref[rows, half:] = wi.astype(BF16)
            lag = (jnp.dot(wr, ctr, precision=HIGHEST, preferred_element_type=F32)
                   - jnp.dot(wi, cti, precision=HIGHEST, preferred_element_type=F32)).astype(BF16)
            for lin in range(CHUNK):
                lout = lin + k
                r = slice(lin * LANES, (lin + 1) * LANES)
                if lout < CHUNK:
                    toep_ref[r, lout * LANES:(lout + 1) * LANES] = lag
                if k > 0:
                    low = lin - k
                    if low >= 0:
                        toep_ref[r, low * LANES:(low + 1) * LANES] = zero
        for l in range(CHUNK):
            arc = acol_ref[:, l + 1:l + 2]
            aic = acol_ref[:, ACOL_IM + l + 1:ACOL_IM + l + 2]
            cols = slice(l * LANES, (l + 1) * LANES)
            wout_ref[:half, cols] = (ctr * arc - cti * aic).astype(BF16)
            wout_ref[half:, cols] = (-(ctr * aic + cti * arc)).astype(BF16)

    x = x_ref[...]
    rows = x.shape[0]
    y = _dot(x, toep_ref[...])
    s_in = _dot(x, win_ref[...])
    sr = s_in[:, :half]
    si = s_in[:, half:]
    cidx = lax.broadcasted_iota(jnp.int32, (rows, 1), 0) % chunks
    for k in range(SCAN_STEPS):
        sh = 1 << k
        ar = arow_ref[SCAN_ROW0 + k:SCAN_ROW0 + k + 1, :half]
        ai = arow_ref[SCAN_ROW0 + k:SCAN_ROW0 + k + 1, half:]
        keep = cidx >= sh
        pr = jnp.where(keep, pltpu.roll(sr, sh, axis=0), 0.0)
        pi = jnp.where(keep, pltpu.roll(si, sh, axis=0), 0.0)
        sr, si = sr + (ar * pr - ai * pi), si + (ar * pi + ai * pr)
    first = cidx >= 1
    pr = jnp.where(first, pltpu.roll(sr, 1, axis=0), 0.0).astype(BF16)
    pi = jnp.where(first, pltpu.roll(si, 1, axis=0), 0.0).astype(BF16)
    y = y + _dot(pr, wout_ref[:half, :]) + _dot(pi, wout_ref[half:, :])
    y = y + d_ref[...] * x.astype(F32)
    o_ref[...] = jax.nn.gelu(y).astype(o_ref.dtype)


def _ssm(xc, btr, bti, ctr, cti, arow, acol, d_tiled, *, chunks, rows=512):
    planes, nrows, width = xc.shape
    nh = nrows // rows

    def plane(*shape):
        return pl.BlockSpec((None,) + shape, lambda j, h: (j,) + (0,) * len(shape))

    return pl.pallas_call(
        functools.partial(_ssm_kernel, chunks=chunks),
        grid=(planes, nh),
        in_specs=[
            pl.BlockSpec((None, rows, width), lambda j, h: (j, h, 0)),
            plane(LANES, HALF_STATE), plane(LANES, HALF_STATE),
            plane(HALF_STATE, LANES), plane(HALF_STATE, LANES),
            plane(APOW_ROWS, PLANE_STATE), plane(HALF_STATE, LANES),
            plane(1, width),
        ],
        out_specs=pl.BlockSpec((None, rows, width), lambda j, h: (j, h, 0)),
        out_shape=jax.ShapeDtypeStruct((planes, nrows, width), BF16),
        scratch_shapes=[pltpu.VMEM((width, width), BF16),
                        pltpu.VMEM((width, PLANE_STATE), BF16),
                        pltpu.VMEM((PLANE_STATE, width), BF16)],
        compiler_params=pltpu.CompilerParams(
            dimension_semantics=("arbitrary", "arbitrary"), vmem_limit_bytes=VMEM_LIMIT),
        name="ssm_scan",
    )(xc, btr, bti, ctr, cti, arow, acol, d_tiled)


def _ssm_tables(lam_re, lam_im, log_dt, b_re, b_im, c_re, c_im, d_skip):
    lam_re = jnp.minimum(lam_re, -1e-4)
    dt = jnp.exp(log_dt)[:, None]
    mag = jnp.exp(lam_re * dt)
    a_re = mag * jnp.cos(lam_im * dt)
    a_im = mag * jnp.sin(lam_im * dt)
    den = lam_re * lam_re + lam_im * lam_im
    p = a_re - 1.0
    f_re = ((p * lam_re + a_im * lam_im) / den)[:, :, None]
    f_im = ((a_im * lam_re - p * lam_im) / den)[:, :, None]
    bb_re = f_re * b_re - f_im * b_im
    bb_im = f_re * b_im + f_im * b_re

    N, C, P, GP, L = SSM_STATE, SSM_GROUP, PLANES, GROUPS_PER_PLANE, CHUNK
    eye = jnp.eye(GP, dtype=F32)

    def bt(bb):
        return jnp.einsum('pq,jpnc->jpcqn', eye, bb.reshape(P, GP, N, C)).reshape(P, LANES, HALF_STATE)

    def ct(c):
        return jnp.einsum('pq,jpon->jpnqo', eye, c.reshape(P, GP, C, N)).reshape(P, HALF_STATE, LANES)

    ks = jnp.concatenate([jnp.arange(L + 1), L * (2 ** jnp.arange(SCAN_STEPS))]).astype(F32)
    ks = ks.reshape(-1, 1, 1)
    m = jnp.exp(ks * (lam_re * dt))
    pr = m * jnp.cos(ks * (lam_im * dt))
    pi = m * jnp.sin(ks * (lam_im * dt))
    arow = jnp.concatenate([pr.reshape(APOW_ROWS, P, HALF_STATE),
                            pi.reshape(APOW_ROWS, P, HALF_STATE)], axis=2)
    arow = jnp.transpose(arow, (1, 0, 2))

    def col(pw):
        c = jnp.transpose(pw[:L + 1].reshape(L + 1, P, HALF_STATE), (1, 2, 0))
        return jnp.pad(c, ((0, 0), (0, 0), (0, ACOL_IM - (L + 1))))

    acol = jnp.concatenate([col(pr), col(pi)], axis=2)
    d_tiled = jnp.tile(d_skip.reshape(P, 1, LANES), (1, 1, L))
    return bt(bb_re), bt(bb_im), ct(c_re), ct(c_im), arow, acol, d_tiled


def _mix_out_kernel(g_ref, yb_ref, gate_ref, h_ref, wglu_ref, bglu_ref, wa_ref, wb_ref, wo_ref, o_ref):
    g = jnp.concatenate([g_ref[j] for j in range(PLANES)], axis=1)
    glu = _dot(g, wglu_ref[...]) + bglu_ref[...]
    y_a = (g.astype(F32) * jax.nn.sigmoid(glu)).astype(BF16)
    z_a = _dot(y_a, wa_ref[...])
    z_b = _dot(yb_ref[...], wb_ref[...])
    merged = (gate_ref[:, :D_MODEL].astype(F32) * z_a
              + gate_ref[:, D_MODEL:].astype(F32) * z_b).astype(BF16)
    o_ref[...] = h_ref[...] + _dot(merged, wo_ref[...])


def _mix_out(g_planes, y_b, gates, h, w_glu, b_glu, w_a, w_b, w_o, *, tm=512):
    t, d = h.shape

    def const(shape):
        return pl.BlockSpec(shape, lambda i: (0,) * len(shape), pipeline_mode=pl.Buffered(1))

    return pl.pallas_call(
        _mix_out_kernel,
        grid=(t // tm,),
        in_specs=[
            pl.BlockSpec((PLANES, tm, LANES), lambda i: (0, i, 0)),
            pl.BlockSpec((tm, CONV_WIDTH), lambda i: (i, 0)),
            pl.BlockSpec((tm, 2 * d), lambda i: (i, 0)),
            pl.BlockSpec((tm, d), lambda i: (i, 0)),
            const((SSM_WIDTH, SSM_WIDTH)), const((1, SSM_WIDTH)),
            const((SSM_WIDTH, d)), const((CONV_WIDTH, d)), const((d, d)),
        ],
        out_specs=pl.BlockSpec((tm, d), lambda i: (i, 0)),
        out_shape=jax.ShapeDtypeStruct((t, d), F32),
        compiler_params=pltpu.CompilerParams(
            dimension_semantics=("parallel",), vmem_limit_bytes=VMEM_LIMIT),
        name="mix_out",
    )(g_planes, y_b, gates, h, w_glu, b_glu, w_a, w_b, w_o)


def kernel(x, ffn1_norm, ffn1_w_gate, ffn1_w_up, ffn1_w_down, mix_norm, w_in, ssm_lambda_re, ssm_lambda_im, ssm_log_dt, ssm_b_re, ssm_b_im, ssm_c_re, ssm_c_im, ssm_d, ssm_w_glu, ssm_b_glu, ssm_w_out, conv_w, conv_b, conv_w_out, w_o, ffn2_norm, ffn2_w_gate, ffn2_w_up, ffn2_w_down, final_norm):
    batch, seq, d = x.shape
    t = batch * seq
    chunks = seq // CHUNK

    bf = lambda w: w.astype(BF16)
    vec = lambda g: g.reshape(1, -1).astype(F32)
    x2 = x.reshape(t, d)

    h1, u = _ffn(x2, vec(ffn1_norm), bf(ffn1_w_gate), bf(ffn1_w_up), bf(ffn1_w_down),
                 vec(mix_norm), emit_hidden=True)

    w_in_b = bf(w_in)
    c0 = SSM_WIDTH
    v_planes = _ssm_in(u, w_in_b[:, :c0])
    y_b = _conv_branch(u, w_in_b[:, c0:c0 + CONV_WIDTH],
                       w_in_b[:, c0 + CONV_WIDTH:c0 + 2 * CONV_WIDTH],
                       w_in_b[:, c0 + 2 * CONV_WIDTH:c0 + 3 * CONV_WIDTH],
                       conv_w.astype(F32), vec(conv_b), batch=batch, seq=seq)
    gates = _gates(u, w_in_b[:, c0 + 3 * CONV_WIDTH:])

    tables = _ssm_tables(ssm_lambda_re, ssm_lambda_im, ssm_log_dt, ssm_b_re, ssm_b_im,
                         ssm_c_re, ssm_c_im, ssm_d)
    xc = v_planes.reshape(PLANES, t // CHUNK, CHUNK * LANES)
    g_planes = _ssm(xc, *tables, chunks=chunks).reshape(PLANES, t, LANES)

    h2 = _mix_out(g_planes, y_b, gates, h1, bf(ssm_w_glu), vec(ssm_b_glu),
                  bf(ssm_w_out), bf(conv_w_out), bf(w_o))

    out = _ffn(h2, vec(ffn2_norm), bf(ffn2_w_gate), bf(ffn2_w_up), bf(ffn2_w_down),
               vec(final_norm), emit_hidden=False)
    return out.reshape(batch, seq, d)
```

```python
import functools

import jax
import jax.numpy as jnp
from jax import lax
from jax.experimental import pallas as pl
from jax.experimental.pallas import tpu as pltpu

D_MODEL = 2048
SSM_WIDTH = 1024
SSM_GROUP = 16
SSM_GROUPS = 64
SSM_STATE = 64
CONV_WIDTH = 1024
CONV_K = 3
EPS = 1e-6

LANES = 128
SUBLANES = 8
CHUNK = 16
PLANES = SSM_WIDTH // LANES
GROUPS_PER_PLANE = LANES // SSM_GROUP
HALF_STATE = GROUPS_PER_PLANE * SSM_STATE
PLANE_STATE = 2 * HALF_STATE
SCAN_STEPS = 7
SCAN_ROW0 = CHUNK + 1
APOW_ROWS = SCAN_ROW0 + SCAN_STEPS
ACOL_IM = LANES // 2
VMEM_LIMIT = 56 * 1024 * 1024
VMEM_LIMIT_FFN = 60 * 1024 * 1024

BF16 = jnp.bfloat16
F32 = jnp.float32
HIGHEST = lax.Precision.HIGHEST


def _rms(x, g):
    return x * lax.rsqrt(jnp.mean(x * x, axis=-1, keepdims=True) + EPS) * g


def _dot(a, b):
    return jnp.dot(a, b, preferred_element_type=F32)


def _dot_f32(a, b):
    return jnp.dot(a, b, precision=HIGHEST, preferred_element_type=F32)


def _ffn_kernel(x_ref, gin_ref, wg_ref, wu_ref, wd_ref, *refs, final_norm, d_ff):
    if final_norm:
        gout_ref, o_ref, xn_ref = refs
    else:
        o_ref, xn_ref = refs
    f = pl.program_id(1)
    tf = wd_ref.shape[0]

    @pl.when(f == 0)
    def _():
        x = x_ref[...]
        xn_ref[...] = _rms(x, gin_ref[...]).astype(BF16)
        o_ref[...] = x

    xn = xn_ref[...]
    gate = _dot(xn, wg_ref[...])
    up = _dot(xn, wu_ref[...])
    valid = d_ff - f * tf
    col = lax.broadcasted_iota(jnp.int32, (1, tf), 1)
    act = jnp.where(col < valid, (0.5 * gate) * jax.nn.sigmoid(gate) * up, 0.0).astype(BF16)
    wrow = lax.broadcasted_iota(jnp.int32, (tf, 1), 0)
    wd = jnp.where(wrow < valid, wd_ref[...], jnp.zeros((), BF16))
    o_ref[...] += _dot(act, wd)

    if final_norm:
        @pl.when(f == pl.num_programs(1) - 1)
        def _():
            o_ref[...] = _rms(o_ref[...], gout_ref[...])


def _ffn(x, g_in, wg, wu, wd, g_out=None, *, tm=1024, tf=512):
    t, d = x.shape
    d_ff = wg.shape[1]
    final_norm = g_out is not None
    row = pl.BlockSpec((tm, d), lambda i, f: (i, 0))
    vec = pl.BlockSpec((1, d), lambda i, f: (0, 0))
    in_specs = [row, vec,
                pl.BlockSpec((d, tf), lambda i, f: (0, f)),
                pl.BlockSpec((d, tf), lambda i, f: (0, f)),
                pl.BlockSpec((tf, d), lambda i, f: (f, 0))]
    args = [x, g_in, wg, wu, wd]
    if final_norm:
        in_specs.append(vec)
        args.append(g_out)
    return pl.pallas_call(
        functools.partial(_ffn_kernel, final_norm=final_norm, d_ff=d_ff),
        grid=(t // tm, pl.cdiv(d_ff, tf)),
        in_specs=in_specs,
        out_specs=row,
        out_shape=jax.ShapeDtypeStruct((t, d), F32),
        scratch_shapes=[pltpu.VMEM((tm, d), BF16)],
        compiler_params=pltpu.CompilerParams(
            dimension_semantics=("parallel", "arbitrary"),
            vmem_limit_bytes=VMEM_LIMIT_FFN),
        name="ffn_final" if final_norm else "ffn_hidden",
    )(*args)


def _in_proj_kernel(h_ref, g_ref, w_ref, cw_ref, cb_ref, v_ref, yb_ref, gate_ref,
                    xn_ref, bg_ref, tail_ref, *, cblk):
    s = pl.program_id(1)
    j = pl.program_id(2)
    ts = h_ref.shape[0]

    @pl.when(j == 0)
    def _():
        xn_ref[...] = _rms(h_ref[...], g_ref[...]).astype(BF16)
        xn = xn_ref[...]
        res = _dot(xn, w_ref[:, :SSM_WIDTH])
        for p in range(PLANES):
            v_ref[p] = res[:, p * LANES:(p + 1) * LANES].astype(v_ref.dtype)
        bg_ref[...] = _dot(xn, w_ref[:, SSM_WIDTH:])

    @pl.when(jnp.logical_and(j == 1, s == 0))
    def _():
        tail_ref[...] = jnp.zeros_like(tail_ref)

    @pl.when(j == 1)
    def _():
        xn = xn_ref[...]
        row = lax.broadcasted_iota(jnp.int32, (ts, 1), 0)
        for c in range(CONV_WIDTH // cblk):
            cs = slice(c * cblk, (c + 1) * cblk)
            vs = slice(CONV_WIDTH + c * cblk, CONV_WIDTH + (c + 1) * cblk)
            z = _dot(xn, w_ref[:, cs]) * _dot(xn, w_ref[:, vs])
            prev1 = tail_ref[SUBLANES - 1:SUBLANES, cs]
            prev2 = tail_ref[SUBLANES - 2:SUBLANES - 1, cs]
            z1 = jnp.where(row == 0, prev1, pltpu.roll(z, 1, axis=0))
            z2 = jnp.where(row == 0, prev2, jnp.where(row == 1, prev1, pltpu.roll(z, 2, axis=0)))
            conv = cb_ref[:, cs] + cw_ref[0:1, cs] * z2 + cw_ref[1:2, cs] * z1 + cw_ref[2:3, cs] * z
            yb_ref[:, cs] = (bg_ref[:, cs] * conv).astype(yb_ref.dtype)
            tail_ref[:, cs] = z[ts - SUBLANES:, :]

    @pl.when(j >= 2)
    def _():
        gate_ref[...] = jax.nn.sigmoid(_dot(xn_ref[...], w_ref[...])).astype(gate_ref.dtype)


def _in_proj(h, g, w, conv_w, conv_b, *, batch, seq, ts=512, cblk=256):
    t, d = h.shape
    nseq = seq // ts
    ngroups = w.shape[1] // d
    tok = lambda b, s, j: (b * nseq + s, 0)
    const = lambda b, s, j: (0, 0)
    return pl.pallas_call(
        functools.partial(_in_proj_kernel, cblk=cblk),
        grid=(batch, nseq, ngroups),
        in_specs=[pl.BlockSpec((ts, d), tok),
                  pl.BlockSpec((1, d), const),
                  pl.BlockSpec((d, d), lambda b, s, j: (0, j)),
                  pl.BlockSpec((CONV_K, CONV_WIDTH), const),
                  pl.BlockSpec((1, CONV_WIDTH), const)],
        out_specs=(pl.BlockSpec((PLANES, ts, LANES), lambda b, s, j: (0, b * nseq + s, 0)),
                   pl.BlockSpec((ts, CONV_WIDTH), tok),
                   pl.BlockSpec((ts, d), lambda b, s, j: (b * nseq + s, jnp.maximum(j - 2, 0)))),
        out_shape=(jax.ShapeDtypeStruct((PLANES, t, LANES), BF16),
                   jax.ShapeDtypeStruct((t, CONV_WIDTH), BF16),
                   jax.ShapeDtypeStruct((t, 2 * d), BF16)),
        scratch_shapes=[pltpu.VMEM((ts, d), BF16),
                        pltpu.VMEM((ts, CONV_WIDTH), F32),
                        pltpu.VMEM((SUBLANES, CONV_WIDTH), F32)],
        compiler_params=pltpu.CompilerParams(
            dimension_semantics=("arbitrary", "arbitrary", "arbitrary"),
            vmem_limit_bytes=VMEM_LIMIT),
        name="in_proj",
    )(h, g, w, conv_w, conv_b)


def _block_diag_tile(src, period, rows_per_group, cols_per_group):
    nrows = src.shape[0]
    ncols = cols_per_group * GROUPS_PER_PLANE
    k = lax.broadcasted_iota(jnp.int32, (src.shape[1], ncols), 0)
    c = lax.broadcasted_iota(jnp.int32, (src.shape[1], ncols), 1)
    tiled = _dot_f32(src, (k == c % period).astype(F32))
    r = lax.broadcasted_iota(jnp.int32, (nrows, ncols), 0)
    c = lax.broadcasted_iota(jnp.int32, (nrows, ncols), 1)
    return jnp.where(r // rows_per_group == c // cols_per_group, tiled, 0.0)


def _ssm_kernel(x_ref, btr_ref, bti_ref, ctr_ref, cti_ref, arow_ref, acol_ref, d_ref, o_ref,
                toep_ref, win_ref, wout_ref, *, chunks):
    half = HALF_STATE

    @pl.when(pl.program_id(1) == 0)
    def _build_operators():
        btr = _block_diag_tile(btr_ref[...], SSM_STATE, SSM_GROUP, SSM_STATE)
        bti = _block_diag_tile(bti_ref[...], SSM_STATE, SSM_GROUP, SSM_STATE)
        ctr = _block_diag_tile(ctr_ref[...], SSM_GROUP, SSM_STATE, SSM_GROUP)
        cti = _block_diag_tile(cti_ref[...], SSM_GROUP, SSM_STATE, SSM_GROUP)
        zero = jnp.zeros((LANES, LANES), BF16)
        for k in range(CHUNK):
            ar = arow_ref[k:k + 1, :half]
            ai = arow_ref[k:k + 1, half:]
            wr = btr * ar - bti * ai
            wi = btr * ai + bti * ar
            rows = slice((CHUNK - 1 - k) * LANES, (CHUNK - k) * LANES)
            win_ref[rows, :half] = wr.astype(BF16)
            win_ref[rows, half:] = wi.astype(BF16)
            lag = (_dot_f32(wr, ctr) - _dot_f32(wi, cti)).astype(BF16)
            for lin in range(CHUNK):
                lout = lin + k
                r = slice(lin * LANES, (lin + 1) * LANES)
                if lout < CHUNK:
                    toep_ref[r, lout * LANES:(lout + 1) * LANES] = lag
                if k > 0:
                    low = lin - k
                    if low >= 0:
                        toep_ref[r, low * LANES:(low + 1) * LANES] = zero
        for l in range(CHUNK):
            arc = acol_ref[:, l + 1:l + 2]
            aic = acol_ref[:, ACOL_IM + l + 1:ACOL_IM + l + 2]
            cols = slice(l * LANES, (l + 1) * LANES)
            wout_ref[:half, cols] = (ctr * arc - cti * aic).astype(BF16)
            wout_ref[half:, cols] = (-(ctr * aic + cti * arc)).astype(BF16)

    x = x_ref[...]
    rows = x.shape[0]
    y = _dot(x, toep_ref[...])
    s_in = _dot(x, win_ref[...])
    sr = s_in[:, :half]
    si = s_in[:, half:]
    cidx = lax.broadcasted_iota(jnp.int32, (rows, 1), 0) % chunks
    for k in range(SCAN_STEPS):
        sh = 1 << k
        ar = arow_ref[SCAN_ROW0 + k:SCAN_ROW0 + k + 1, :half]
        ai = arow_ref[SCAN_ROW0 + k:SCAN_ROW0 + k + 1, half:]
        keep = cidx >= sh
        pr = jnp.where(keep, pltpu.roll(sr, sh, axis=0), 0.0)
        pi = jnp.where(keep, pltpu.roll(si, sh, axis=0), 0.0)
        sr, si = sr + (ar * pr - ai * pi), si + (ar * pi + ai * pr)
    first = cidx >= 1
    pr = jnp.where(first, pltpu.roll(sr, 1, axis=0), 0.0).astype(BF16)
    pi = jnp.where(first, pltpu.roll(si, 1, axis=0), 0.0).astype(BF16)
    y = y + _dot(pr, wout_ref[:half, :]) + _dot(pi, wout_ref[half:, :])
    y = y + d_ref[...] * x.astype(F32)
    o_ref[...] = jax.nn.gelu(y).astype(o_ref.dtype)


def _ssm(xc, btr, bti, ctr, cti, arow, acol, d_tiled, *, chunks, rows=512):
    planes, nrows, width = xc.shape
    nh = nrows // rows

    def plane(*shape):
        return pl.BlockSpec((None,) + shape, lambda j, h: (j,) + (0,) * len(shape))

    return pl.pallas_call(
        functools.partial(_ssm_kernel, chunks=chunks),
        grid=(planes, nh),
        in_specs=[
            pl.BlockSpec((None, rows, width), lambda j, h: (j, h, 0)),
            plane(LANES, LANES), plane(LANES, LANES),
            plane(HALF_STATE, LANES), plane(HALF_STATE, LANES),
            plane(APOW_ROWS, PLANE_STATE), plane(HALF_STATE, LANES),
            plane(1, width),
        ],
        out_specs=pl.BlockSpec((None, rows, width), lambda j, h: (j, h, 0)),
        out_shape=jax.ShapeDtypeStruct((planes, nrows, width), BF16),
        scratch_shapes=[pltpu.VMEM((width, width), BF16),
                        pltpu.VMEM((width, PLANE_STATE), BF16),
                        pltpu.VMEM((PLANE_STATE, width), BF16)],
        compiler_params=pltpu.CompilerParams(
            dimension_semantics=("arbitrary", "arbitrary"), vmem_limit_bytes=VMEM_LIMIT),
        name="ssm_scan",
    )(xc, btr, bti, ctr, cti, arow, acol, d_tiled)


def _ssm_tables(lam_re, lam_im, log_dt, b_re, b_im, c_re, c_im, d_skip):
    lam_re = jnp.minimum(lam_re, -1e-4)
    dt = jnp.exp(log_dt)[:, None]
    mag = jnp.exp(lam_re * dt)
    a_re = mag * jnp.cos(lam_im * dt)
    a_im = mag * jnp.sin(lam_im * dt)
    den = lam_re * lam_re + lam_im * lam_im
    p = a_re - 1.0
    f_re = ((p * lam_re + a_im * lam_im) / den)[:, :, None]
    f_im = ((a_im * lam_re - p * lam_im) / den)[:, :, None]
    bb_re = f_re * b_re - f_im * b_im
    bb_im = f_re * b_im + f_im * b_re

    N, C, P, L = SSM_STATE, SSM_GROUP, PLANES, CHUNK

    def bt(bb):
        m = jnp.transpose(bb, (0, 2, 1)).reshape(P, LANES, N)
        return jnp.pad(m, ((0, 0), (0, 0), (0, LANES - N)))

    def ct(c):
        m = jnp.transpose(c, (0, 2, 1)).reshape(P, HALF_STATE, C)
        return jnp.pad(m, ((0, 0), (0, 0), (0, LANES - C)))

    ks = jnp.concatenate([jnp.arange(L + 1), L * (2 ** jnp.arange(SCAN_STEPS))]).astype(F32)
    ks = ks.reshape(-1, 1, 1)
    m = jnp.exp(ks * (lam_re * dt))
    pr = m * jnp.cos(ks * (lam_im * dt))
    pi = m * jnp.sin(ks * (lam_im * dt))
    arow = jnp.concatenate([pr.reshape(APOW_ROWS, P, HALF_STATE),
                            pi.reshape(APOW_ROWS, P, HALF_STATE)], axis=2)
    arow = jnp.transpose(arow, (1, 0, 2))

    def col(pw):
        c = jnp.transpose(pw[:L + 1].reshape(L + 1, P, HALF_STATE), (1, 2, 0))
        return jnp.pad(c, ((0, 0), (0, 0), (0, ACOL_IM - (L + 1))))

    acol = jnp.concatenate([col(pr), col(pi)], axis=2)
    d_tiled = jnp.tile(d_skip.reshape(P, 1, LANES), (1, 1, L))
    return bt(bb_re), bt(bb_im), ct(c_re), ct(c_im), arow, acol, d_tiled


def _mix_out_kernel(g_ref, yb_ref, gate_ref, h_ref, wglu_ref, bglu_ref, wa_ref, wb_ref, wo_ref, o_ref):
    g = jnp.concatenate([g_ref[j] for j in range(PLANES)], axis=1)
    glu = _dot(g, wglu_ref[...]) + bglu_ref[...]
    y_a = (g.astype(F32) * jax.nn.sigmoid(glu)).astype(BF16)
    z_a = _dot(y_a, wa_ref[...])
    z_b = _dot(yb_ref[...], wb_ref[...])
    merged = (gate_ref[:, :D_MODEL].astype(F32) * z_a
              + gate_ref[:, D_MODEL:].astype(F32) * z_b).astype(BF16)
    o_ref[...] = h_ref[...] + _dot(merged, wo_ref[...])


def _mix_out(g_planes, y_b, gates, h, w_glu, b_glu, w_a, w_b, w_o, *, tm=512):
    t, d = h.shape

    def const(shape):
        return pl.BlockSpec(shape, lambda i: (0,) * len(shape), pipeline_mode=pl.Buffered(1))

    return pl.pallas_call(
        _mix_out_kernel,
        grid=(t // tm,),
        in_specs=[
            pl.BlockSpec((PLANES, tm, LANES), lambda i: (0, i, 0)),
            pl.BlockSpec((tm, CONV_WIDTH), lambda i: (i, 0)),
            pl.BlockSpec((tm, 2 * d), lambda i: (i, 0)),
            pl.BlockSpec((tm, d), lambda i: (i, 0)),
            const((SSM_WIDTH, SSM_WIDTH)), const((1, SSM_WIDTH)),
            const((SSM_WIDTH, d)), const((CONV_WIDTH, d)), const((d, d)),
        ],
        out_specs=pl.BlockSpec((tm, d), lambda i: (i, 0)),
        out_shape=jax.ShapeDtypeStruct((t, d), F32),
        compiler_params=pltpu.CompilerParams(
            dimension_semantics=("parallel",), vmem_limit_bytes=VMEM_LIMIT),
        name="mix_out",
    )(g_planes, y_b, gates, h, w_glu, b_glu, w_a, w_b, w_o)


def kernel(x, ffn1_norm, ffn1_w_gate, ffn1_w_up, ffn1_w_down, mix_norm, w_in, ssm_lambda_re, ssm_lambda_im, ssm_log_dt, ssm_b_re, ssm_b_im, ssm_c_re, ssm_c_im, ssm_d, ssm_w_glu, ssm_b_glu, ssm_w_out, conv_w, conv_b, conv_w_out, w_o, ffn2_norm, ffn2_w_gate, ffn2_w_up, ffn2_w_down, final_norm):
    batch, seq, d = x.shape
    t = batch * seq
    chunks = seq // CHUNK

    bf = lambda w: w.astype(BF16)
    vec = lambda g: g.reshape(1, -1).astype(F32)

    h1 = _ffn(x.reshape(t, d), vec(ffn1_norm), bf(ffn1_w_gate), bf(ffn1_w_up), bf(ffn1_w_down))

    v_planes, y_b, gates = _in_proj(h1, vec(mix_norm), bf(w_in), conv_w.astype(F32), vec(conv_b),
                                    batch=batch, seq=seq)

    tables = _ssm_tables(ssm_lambda_re, ssm_lambda_im, ssm_log_dt, ssm_b_re, ssm_b_im,
                         ssm_c_re, ssm_c_im, ssm_d)
    xc = v_planes.reshape(PLANES, t // CHUNK, CHUNK * LANES)
    g_planes = _ssm(xc, *tables, chunks=chunks).reshape(PLANES, t, LANES)

    h2 = _mix_out(g_planes, y_b, gates, h1, bf(ssm_w_glu), vec(ssm_b_glu),
                  bf(ssm_w_out), bf(conv_w_out), bf(w_o))

    out = _ffn(h2, vec(ffn2_norm), bf(ffn2_w_gate), bf(ffn2_w_up), bf(ffn2_w_down), vec(final_norm))
    return out.reshape(batch, seq, d)
```

```python
import functools

import jax
import jax.numpy as jnp
from jax import lax
from jax.experimental import pallas as pl
from jax.experimental.pallas import tpu as pltpu

D_MODEL = 2048
SSM_WIDTH = 1024
SSM_GROUP = 16
SSM_GROUPS = 64
SSM_STATE = 64
CONV_WIDTH = 1024
CONV_K = 3
EPS = 1e-6

LANES = 128
SUBLANES = 8
MXU_DIM = 256
CHUNK = 16
PLANES = SSM_WIDTH // LANES
GROUPS_PER_PLANE = LANES // SSM_GROUP
HALF_STATE = GROUPS_PER_PLANE * SSM_STATE
PLANE_STATE = 2 * HALF_STATE
SCAN_STEPS = 7
SCAN_ROW0 = CHUNK + 1
APOW_ROWS = SCAN_ROW0 + SCAN_STEPS
ACOL_IM = LANES // 2
VMEM_LIMIT = 56 * 1024 * 1024
VMEM_LIMIT_FFN = 60 * 1024 * 1024

BF16 = jnp.bfloat16
F32 = jnp.float32
HIGHEST = lax.Precision.HIGHEST


def _rms(x, g):
    return x * lax.rsqrt(jnp.mean(x * x, axis=-1, keepdims=True) + EPS) * g


def _dot(a, b):
    return jnp.dot(a, b, preferred_element_type=F32)


def _dot_f32(a, b):
    return jnp.dot(a, b, precision=HIGHEST, preferred_element_type=F32)


def _ffn_kernel(x_ref, gin_ref, wg_ref, wu_ref, wd_ref, *refs, final_norm, d_ff):
    if final_norm:
        gout_ref, o_ref, xn_ref = refs
    else:
        o_ref, xn_ref = refs
    f = pl.program_id(1)
    nf = pl.num_programs(1)
    tf = wd_ref.shape[0]

    @pl.when(f == 0)
    def _():
        x = x_ref[...]
        xn_ref[...] = _rms(x, gin_ref[...]).astype(BF16)
        o_ref[...] = x

    def accumulate(width):
        xn = xn_ref[...]
        gate = _dot(xn, wg_ref[:, :width])
        up = _dot(xn, wu_ref[:, :width])
        act = ((0.5 * gate) * jax.nn.sigmoid(gate) * up).astype(BF16)
        o_ref[...] += _dot(act, wd_ref[:width, :])

    last_width = d_ff - (pl.cdiv(d_ff, tf) - 1) * tf
    if last_width == tf:
        accumulate(tf)
    else:
        pl.when(f < nf - 1)(functools.partial(accumulate, tf))
        pl.when(f == nf - 1)(functools.partial(accumulate, last_width))

    if final_norm:
        @pl.when(f == nf - 1)
        def _():
            o_ref[...] = _rms(o_ref[...], gout_ref[...])


def _ffn(x, g_in, wg, wu, wd, g_out=None, *, tm=1024, tf=512):
    t, d = x.shape
    d_ff = wg.shape[1]
    final_norm = g_out is not None
    row = pl.BlockSpec((tm, d), lambda i, f: (i, 0))
    vec = pl.BlockSpec((1, d), lambda i, f: (0, 0))
    in_specs = [row, vec,
                pl.BlockSpec((d, tf), lambda i, f: (0, f)),
                pl.BlockSpec((d, tf), lambda i, f: (0, f)),
                pl.BlockSpec((tf, d), lambda i, f: (f, 0))]
    args = [x, g_in, wg, wu, wd]
    if final_norm:
        in_specs.append(vec)
        args.append(g_out)
    return pl.pallas_call(
        functools.partial(_ffn_kernel, final_norm=final_norm, d_ff=d_ff),
        grid=(t // tm, pl.cdiv(d_ff, tf)),
        in_specs=in_specs,
        out_specs=row,
        out_shape=jax.ShapeDtypeStruct((t, d), F32),
        scratch_shapes=[pltpu.VMEM((tm, d), BF16)],
        compiler_params=pltpu.CompilerParams(
            dimension_semantics=("parallel", "arbitrary"),
            vmem_limit_bytes=VMEM_LIMIT_FFN),
        name="ffn_final" if final_norm else "ffn_hidden",
    )(*args)


GATE_GROUP0 = 4


def _store_planes(ref, val):
    for p in range(ref.shape[0]):
        ref[p] = val[:, p * LANES:(p + 1) * LANES]


def _load_planes(ref, p0, p1):
    return jnp.concatenate([ref[p] for p in range(p0, p1)], axis=1)


def _in_proj_kernel(h_ref, g_ref, w_ref, cw_ref, cb_ref, v_ref, yb_ref, gate_ref,
                    xn_ref, bg_ref, stage_ref, tail_ref, *, cblk):
    s = pl.program_id(1)
    j = pl.program_id(2)
    ts = h_ref.shape[0]
    nchunk = ts // CHUNK

    @pl.when(j == 0)
    def _ssm_input():
        xn_ref[...] = _rms(h_ref[...], g_ref[...]).astype(BF16)
        _store_planes(stage_ref, _dot(xn_ref[...], w_ref[...]))
        for p in range(PLANES):
            for l in range(CHUNK):
                piece = stage_ref[p, pl.ds(l, nchunk, stride=CHUNK), :]
                v_ref[p, :, l * LANES:(l + 1) * LANES] = piece.astype(v_ref.dtype)

    @pl.when(j == 1)
    def _b_gate():
        bg_ref[...] = _dot(xn_ref[...], w_ref[...])

    @pl.when(j == 2)
    def _c_gate():
        _store_planes(stage_ref, _dot(xn_ref[...], w_ref[...]))

    @pl.when(jnp.logical_and(j == 3, s == 0))
    def _():
        tail_ref[...] = jnp.zeros_like(tail_ref)

    @pl.when(j == 3)
    def _conv():
        xn = xn_ref[...]
        row = lax.broadcasted_iota(jnp.int32, (ts, 1), 0)
        for c in range(CONV_WIDTH // cblk):
            cs = slice(c * cblk, (c + 1) * cblk)
            c_gate = _load_planes(stage_ref, c * cblk // LANES, (c + 1) * cblk // LANES)
            z = c_gate * _dot(xn, w_ref[:, cs])
            prev1 = tail_ref[SUBLANES - 1:SUBLANES, cs]
            prev2 = tail_ref[SUBLANES - 2:SUBLANES - 1, cs]
            z1 = jnp.where(row == 0, prev1, pltpu.roll(z, 1, axis=0))
            z2 = jnp.where(row == 0, prev2, jnp.where(row == 1, prev1, pltpu.roll(z, 2, axis=0)))
            conv = cb_ref[:, cs] + cw_ref[0:1, cs] * z2 + cw_ref[1:2, cs] * z1 + cw_ref[2:3, cs] * z
            yb_ref[:, cs] = (bg_ref[:, cs] * conv).astype(yb_ref.dtype)
            tail_ref[:, cs] = z[ts - SUBLANES:, :]

    @pl.when(j >= GATE_GROUP0)
    def _gates():
        gate_ref[...] = jax.nn.sigmoid(_dot(xn_ref[...], w_ref[...])).astype(gate_ref.dtype)


def _in_proj(h, g, w, conv_w, conv_b, *, batch, seq, ts=1024, cblk=256):
    t, d = h.shape
    nseq = seq // ts
    gw = CONV_WIDTH
    ngroups = w.shape[1] // gw
    tok = lambda b, s, j: (b * nseq + s, 0)
    const = lambda b, s, j: (0, 0)
    return pl.pallas_call(
        functools.partial(_in_proj_kernel, cblk=cblk),
        grid=(batch, nseq, ngroups),
        in_specs=[pl.BlockSpec((ts, d), tok),
                  pl.BlockSpec((1, d), const),
                  pl.BlockSpec((d, gw), lambda b, s, j: (0, j)),
                  pl.BlockSpec((CONV_K, CONV_WIDTH), const),
                  pl.BlockSpec((1, CONV_WIDTH), const)],
        out_specs=(pl.BlockSpec((PLANES, ts // CHUNK, CHUNK * LANES),
                                lambda b, s, j: (0, b * nseq + s, 0)),
                   pl.BlockSpec((ts, CONV_WIDTH), tok),
                   pl.BlockSpec((ts, gw),
                                lambda b, s, j: (b * nseq + s, jnp.maximum(j - GATE_GROUP0, 0)))),
        out_shape=(jax.ShapeDtypeStruct((PLANES, t // CHUNK, CHUNK * LANES), BF16),
                   jax.ShapeDtypeStruct((t, CONV_WIDTH), BF16),
                   jax.ShapeDtypeStruct((t, 2 * d), BF16)),
        scratch_shapes=[pltpu.VMEM((ts, d), BF16),
                        pltpu.VMEM((ts, CONV_WIDTH), F32),
                        pltpu.VMEM((PLANES, ts, LANES), F32),
                        pltpu.VMEM((SUBLANES, CONV_WIDTH), F32)],
        compiler_params=pltpu.CompilerParams(
            dimension_semantics=("arbitrary", "arbitrary", "arbitrary"),
            vmem_limit_bytes=VMEM_LIMIT),
        name="in_proj",
    )(h, g, w, conv_w, conv_b)


def _block_diag_tile(src, period, rows_per_group, cols_per_group):
    nrows = src.shape[0]
    ncols = cols_per_group * GROUPS_PER_PLANE
    k = lax.broadcasted_iota(jnp.int32, (src.shape[1], ncols), 0)
    c = lax.broadcasted_iota(jnp.int32, (src.shape[1], ncols), 1)
    tiled = _dot_f32(src, (k == c % period).astype(F32))
    r = lax.broadcasted_iota(jnp.int32, (nrows, ncols), 0)
    c = lax.broadcasted_iota(jnp.int32, (nrows, ncols), 1)
    return jnp.where(r // rows_per_group == c // cols_per_group, tiled, 0.0)


def _ssm_kernel(x_ref, btr_ref, bti_ref, ctr_ref, cti_ref, arow_ref, acol_ref, d_ref, o_ref,
                toep_ref, win_ref, wout_ref, *, chunks):
    half = HALF_STATE

    @pl.when(pl.program_id(1) == 0)
    def _build_operators():
        btr = _block_diag_tile(btr_ref[...], SSM_STATE, SSM_GROUP, SSM_STATE)
        bti = _block_diag_tile(bti_ref[...], SSM_STATE, SSM_GROUP, SSM_STATE)
        ctr = _block_diag_tile(ctr_ref[...], SSM_GROUP, SSM_STATE, SSM_GROUP)
        cti = _block_diag_tile(cti_ref[...], SSM_GROUP, SSM_STATE, SSM_GROUP)
        zero = jnp.zeros((LANES, LANES), BF16)
        for k in range(CHUNK):
            ar = arow_ref[k:k + 1, :half]
            ai = arow_ref[k:k + 1, half:]
            wr = btr * ar - bti * ai
            wi = btr * ai + bti * ar
            rows = slice((CHUNK - 1 - k) * LANES, (CHUNK - k) * LANES)
            win_ref[rows, :half] = wr.astype(BF16)
            win_ref[rows, half:] = wi.astype(BF16)
            lag = (_dot_f32(wr, ctr) - _dot_f32(wi, cti)).astype(BF16)
            for lin in range(CHUNK - k):
                lout = lin + k
                toep_ref[lin * LANES:(lin + 1) * LANES, lout * LANES:(lout + 1) * LANES] = lag
        for lin in range(1, CHUNK, MXU_DIM // LANES):
            toep_ref[lin * LANES:(lin + 1) * LANES, (lin - 1) * LANES:lin * LANES] = zero
        for l in range(CHUNK):
            arc = acol_ref[:, l + 1:l + 2]
            aic = acol_ref[:, ACOL_IM + l + 1:ACOL_IM + l + 2]
            cols = slice(l * LANES, (l + 1) * LANES)
            wout_ref[:half, cols] = (ctr * arc - cti * aic).astype(BF16)
            wout_ref[half:, cols] = (-(ctr * aic + cti * arc)).astype(BF16)

    rows, width = x_ref.shape
    s_in = _dot(x_ref[...], win_ref[...])
    sr = s_in[:, :half]
    si = s_in[:, half:]
    cidx = lax.broadcasted_iota(jnp.int32, (rows, 1), 0) % chunks
    for k in range(SCAN_STEPS):
        sh = 1 << k
        ar = arow_ref[SCAN_ROW0 + k:SCAN_ROW0 + k + 1, :half]
        ai = arow_ref[SCAN_ROW0 + k:SCAN_ROW0 + k + 1, half:]
        keep = cidx >= sh
        pr = jnp.where(keep, pltpu.roll(sr, sh, axis=0), 0.0)
        pi = jnp.where(keep, pltpu.roll(si, sh, axis=0), 0.0)
        sr, si = sr + (ar * pr - ai * pi), si + (ar * pi + ai * pr)
    first = cidx >= 1
    pr = jnp.where(first, pltpu.roll(sr, 1, axis=0), 0.0).astype(BF16)
    pi = jnp.where(first, pltpu.roll(si, 1, axis=0), 0.0).astype(BF16)
    s_prev = jnp.concatenate([pr, pi], axis=1)
    for cb in range(width // MXU_DIM):
        cols = slice(cb * MXU_DIM, (cb + 1) * MXU_DIM)
        kk = (cb + 1) * MXU_DIM
        y = _dot(x_ref[:, :kk], toep_ref[:kk, cols]) + _dot(s_prev, wout_ref[:, cols])
        y = y + d_ref[:, cols] * x_ref[:, cols].astype(F32)
        o_ref[:, cols] = jax.nn.gelu(y).astype(o_ref.dtype)


def _ssm(xc, btr, bti, ctr, cti, arow, acol, d_tiled, *, chunks, rows=512):
    planes, nrows, width = xc.shape
    nh = nrows // rows

    def plane(*shape):
        return pl.BlockSpec((None,) + shape, lambda j, h: (j,) + (0,) * len(shape))

    return pl.pallas_call(
        functools.partial(_ssm_kernel, chunks=chunks),
        grid=(planes, nh),
        in_specs=[
            pl.BlockSpec((None, rows, width), lambda j, h: (j, h, 0)),
            plane(LANES, LANES), plane(LANES, LANES),
            plane(HALF_STATE, LANES), plane(HALF_STATE, LANES),
            plane(APOW_ROWS, PLANE_STATE), plane(HALF_STATE, LANES),
            plane(1, width),
        ],
        out_specs=pl.BlockSpec((None, rows, width), lambda j, h: (j, h, 0)),
        out_shape=jax.ShapeDtypeStruct((planes, nrows, width), BF16),
        scratch_shapes=[pltpu.VMEM((width, width), BF16),
                        pltpu.VMEM((width, PLANE_STATE), BF16),
                        pltpu.VMEM((PLANE_STATE, width), BF16)],
        compiler_params=pltpu.CompilerParams(
            dimension_semantics=("arbitrary", "arbitrary"), vmem_limit_bytes=VMEM_LIMIT),
        name="ssm_scan",
    )(xc, btr, bti, ctr, cti, arow, acol, d_tiled)


def _ssm_tables(lam_re, lam_im, log_dt, b_re, b_im, c_re, c_im, d_skip):
    lam_re = jnp.minimum(lam_re, -1e-4)
    dt = jnp.exp(log_dt)[:, None]
    mag = jnp.exp(lam_re * dt)
    a_re = mag * jnp.cos(lam_im * dt)
    a_im = mag * jnp.sin(lam_im * dt)
    den = lam_re * lam_re + lam_im * lam_im
    p = a_re - 1.0
    f_re = ((p * lam_re + a_im * lam_im) / den)[:, :, None]
    f_im = ((a_im * lam_re - p * lam_im) / den)[:, :, None]
    bb_re = f_re * b_re - f_im * b_im
    bb_im = f_re * b_im + f_im * b_re

    N, C, P, L = SSM_STATE, SSM_GROUP, PLANES, CHUNK

    def bt(bb):
        m = jnp.transpose(bb, (0, 2, 1)).reshape(P, LANES, N)
        return jnp.pad(m, ((0, 0), (0, 0), (0, LANES - N)))

    def ct(c):
        m = jnp.transpose(c, (0, 2, 1)).reshape(P, HALF_STATE, C)
        return jnp.pad(m, ((0, 0), (0, 0), (0, LANES - C)))

    ks = jnp.concatenate([jnp.arange(L + 1), L * (2 ** jnp.arange(SCAN_STEPS))]).astype(F32)
    ks = ks.reshape(-1, 1, 1)
    m = jnp.exp(ks * (lam_re * dt))
    pr = m * jnp.cos(ks * (lam_im * dt))
    pi = m * jnp.sin(ks * (lam_im * dt))
    arow = jnp.concatenate([pr.reshape(APOW_ROWS, P, HALF_STATE),
                            pi.reshape(APOW_ROWS, P, HALF_STATE)], axis=2)
    arow = jnp.transpose(arow, (1, 0, 2))

    def col(pw):
        c = jnp.transpose(pw[:L + 1].reshape(L + 1, P, HALF_STATE), (1, 2, 0))
        return jnp.pad(c, ((0, 0), (0, 0), (0, ACOL_IM - (L + 1))))

    acol = jnp.concatenate([col(pr), col(pi)], axis=2)
    d_tiled = jnp.tile(d_skip.reshape(P, 1, LANES), (1, 1, L))
    return bt(bb_re), bt(bb_im), ct(c_re), ct(c_im), arow, acol, d_tiled


def _mix_out_kernel(g_ref, yb_ref, gate_ref, h_ref, wglu_ref, bglu_ref, wa_ref, wb_ref, wo_ref, o_ref,
                    gs_ref):
    nchunk = g_ref.shape[1]
    for p in range(PLANES):
        for l in range(CHUNK):
            piece = g_ref[p, :, l * LANES:(l + 1) * LANES].astype(F32)
            gs_ref[p, pl.ds(l, nchunk, stride=CHUNK), :] = piece
    g = _load_planes(gs_ref, 0, PLANES)
    glu = _dot(g.astype(BF16), wglu_ref[...]) + bglu_ref[...]
    y_a = (g * jax.nn.sigmoid(glu)).astype(BF16)
    z_a = _dot(y_a, wa_ref[...])
    z_b = _dot(yb_ref[...], wb_ref[...])
    merged = (gate_ref[:, :D_MODEL].astype(F32) * z_a
              + gate_ref[:, D_MODEL:].astype(F32) * z_b).astype(BF16)
    o_ref[...] = h_ref[...] + _dot(merged, wo_ref[...])


def _mix_out(g_planes, y_b, gates, h, w_glu, b_glu, w_a, w_b, w_o, *, tm=512):
    t, d = h.shape

    def const(shape):
        return pl.BlockSpec(shape, lambda i: (0,) * len(shape), pipeline_mode=pl.Buffered(1))

    return pl.pallas_call(
        _mix_out_kernel,
        grid=(t // tm,),
        in_specs=[
            pl.BlockSpec((PLANES, tm // CHUNK, CHUNK * LANES), lambda i: (0, i, 0)),
            pl.BlockSpec((tm, CONV_WIDTH), lambda i: (i, 0)),
            pl.BlockSpec((tm, 2 * d), lambda i: (i, 0)),
            pl.BlockSpec((tm, d), lambda i: (i, 0)),
            const((SSM_WIDTH, SSM_WIDTH)), const((1, SSM_WIDTH)),
            const((SSM_WIDTH, d)), const((CONV_WIDTH, d)), const((d, d)),
        ],
        out_specs=pl.BlockSpec((tm, d), lambda i: (i, 0)),
        out_shape=jax.ShapeDtypeStruct((t, d), F32),
        scratch_shapes=[pltpu.VMEM((PLANES, tm, LANES), F32)],
        compiler_params=pltpu.CompilerParams(
            dimension_semantics=("parallel",), vmem_limit_bytes=VMEM_LIMIT),
        name="mix_out",
    )(g_planes, y_b, gates, h, w_glu, b_glu, w_a, w_b, w_o)


def kernel(x, ffn1_norm, ffn1_w_gate, ffn1_w_up, ffn1_w_down, mix_norm, w_in, ssm_lambda_re, ssm_lambda_im, ssm_log_dt, ssm_b_re, ssm_b_im, ssm_c_re, ssm_c_im, ssm_d, ssm_w_glu, ssm_b_glu, ssm_w_out, conv_w, conv_b, conv_w_out, w_o, ffn2_norm, ffn2_w_gate, ffn2_w_up, ffn2_w_down, final_norm):
    batch, seq, d = x.shape
    t = batch * seq
    chunks = seq // CHUNK

    bf = lambda w: w.astype(BF16)
    vec = lambda g: g.reshape(1, -1).astype(F32)

    h1 = _ffn(x.reshape(t, d), vec(ffn1_norm), bf(ffn1_w_gate), bf(ffn1_w_up), bf(ffn1_w_down))

    v_chunks, y_b, gates = _in_proj(h1, vec(mix_norm), bf(w_in), conv_w.astype(F32), vec(conv_b),
                                    batch=batch, seq=seq)

    tables = _ssm_tables(ssm_lambda_re, ssm_lambda_im, ssm_log_dt, ssm_b_re, ssm_b_im,
                         ssm_c_re, ssm_c_im, ssm_d)
    g_planes = _ssm(v_chunks, *tables, chunks=chunks)

    h2 = _mix_out(g_planes, y_b, gates, h1, bf(ssm_w_glu), vec(ssm_b_glu),
                  bf(ssm_w_out), bf(conv_w_out), bf(w_o))

    out = _ffn(h2, vec(ffn2_norm), bf(ffn2_w_gate), bf(ffn2_w_up), bf(ffn2_w_down), vec(final_norm))
    return out.reshape(batch, seq, d)
```

```python
import functools

import jax
import jax.numpy as jnp
from jax import lax
from jax.experimental import pallas as pl
from jax.experimental.pallas import tpu as pltpu

D_MODEL = 2048
SSM_WIDTH = 1024
SSM_GROUP = 16
SSM_GROUPS = 64
SSM_STATE = 64
CONV_WIDTH = 1024
CONV_K = 3
EPS = 1e-6

LANES = 128
SUBLANES = 8
BF16_SUBLANES = 16
MXU_DIM = 256
CHUNK = 16
PLANES = SSM_WIDTH // LANES
GROUPS_PER_PLANE = LANES // SSM_GROUP
HALF_STATE = GROUPS_PER_PLANE * SSM_STATE
PLANE_STATE = 2 * HALF_STATE
SCAN_STEPS = 7
SCAN_ROW0 = CHUNK + 1
APOW_ROWS = SCAN_ROW0 + SCAN_STEPS
ACOL_IM = LANES // 2
VMEM_LIMIT = 56 * 1024 * 1024
VMEM_LIMIT_FFN = 60 * 1024 * 1024

BF16 = jnp.bfloat16
F32 = jnp.float32
HIGHEST = lax.Precision.HIGHEST


def _rms(x, g):
    return x * lax.rsqrt(jnp.mean(x * x, axis=-1, keepdims=True) + EPS) * g


def _dot(a, b):
    return jnp.dot(a, b, preferred_element_type=F32)


def _dot_f32(a, b):
    return jnp.dot(a, b, precision=HIGHEST, preferred_element_type=F32)


def _ffn_kernel(x_ref, gin_ref, wg_ref, wu_ref, wd_ref, *refs, final_norm, d_ff, n_cast):
    if final_norm:
        gout_ref, *refs = refs
    cast_src = refs[:n_cast]
    o_ref = refs[n_cast]
    cast_dst = refs[n_cast + 1:2 * n_cast + 1]
    xn_ref = refs[2 * n_cast + 1]
    f = pl.program_id(1)
    nf = pl.num_programs(1)
    tf = wd_ref.shape[0]

    @pl.when(f == 0)
    def _():
        x = x_ref[...]
        xn_ref[...] = _rms(x, gin_ref[...]).astype(BF16)
        o_ref[...] = x

    def accumulate(width):
        xn = xn_ref[...]
        gate = _dot(xn, wg_ref[:, :width])
        up = _dot(xn, wu_ref[:, :width])
        act = ((0.5 * gate) * jax.nn.sigmoid(gate) * up).astype(BF16)
        o_ref[...] += _dot(act, wd_ref[:width, :])

    last_width = d_ff - (pl.cdiv(d_ff, tf) - 1) * tf
    if last_width == tf:
        accumulate(tf)
    else:
        pl.when(f < nf - 1)(functools.partial(accumulate, tf))
        pl.when(f == nf - 1)(functools.partial(accumulate, last_width))

    if final_norm:
        @pl.when(f == nf - 1)
        def _():
            o_ref[...] = _rms(o_ref[...], gout_ref[...])

    for src, dst in zip(cast_src, cast_dst):
        dst[...] = src[...].astype(dst.dtype)


def _cast_block_rows(nrows, steps):
    br = BF16_SUBLANES
    while nrows % br or nrows // br > steps:
        br += BF16_SUBLANES
    return br


def _ffn(x, g_in, wg, wu, wd, g_out=None, *, cast=(), tm=1024, tf=512):
    t, d = x.shape
    d_ff = wg.shape[1]
    final_norm = g_out is not None
    nf = pl.cdiv(d_ff, tf)
    steps = (t // tm) * nf
    row = pl.BlockSpec((tm, d), lambda i, f: (i, 0))
    vec = pl.BlockSpec((1, d), lambda i, f: (0, 0))
    in_specs = [row, vec,
                pl.BlockSpec((d, tf), lambda i, f: (0, f)),
                pl.BlockSpec((d, tf), lambda i, f: (0, f)),
                pl.BlockSpec((tf, d), lambda i, f: (f, 0))]
    args = [x, g_in, wg, wu, wd]
    if final_norm:
        in_specs.append(vec)
        args.append(g_out)
    def cast_specs():
        specs = []
        for w in cast:
            nrows, ncols = w.shape
            br = _cast_block_rows(nrows, steps)
            specs.append(pl.BlockSpec(
                (br, ncols), lambda i, f, nb=nrows // br: (jnp.minimum(i * nf + f, nb - 1), 0)))
        return specs

    outs = pl.pallas_call(
        functools.partial(_ffn_kernel, final_norm=final_norm, d_ff=d_ff, n_cast=len(cast)),
        grid=(t // tm, nf),
        in_specs=in_specs + cast_specs(),
        out_specs=[row] + cast_specs(),
        out_shape=[jax.ShapeDtypeStruct((t, d), F32)]
        + [jax.ShapeDtypeStruct(w.shape, BF16) for w in cast],
        scratch_shapes=[pltpu.VMEM((tm, d), BF16)],
        compiler_params=pltpu.CompilerParams(
            dimension_semantics=("arbitrary", "arbitrary"),
            vmem_limit_bytes=VMEM_LIMIT_FFN),
        name="ffn_final" if final_norm else "ffn_hidden",
    )(*args, *cast)
    return outs if cast else outs[0]


GATE_GROUP0 = 2


def _store_planes(ref, val):
    for p in range(ref.shape[0]):
        ref[p] = val[:, p * LANES:(p + 1) * LANES]


def _load_planes(ref, p0, p1):
    return jnp.concatenate([ref[p] for p in range(p0, p1)], axis=1)


def _in_proj_kernel(h_ref, g_ref, w_ref, cw_ref, cb_ref, v_ref, yb_ref, gate_ref,
                    xn_ref, bg_ref, stage_ref, tail_ref, *, cblk):
    s = pl.program_id(1)
    j = pl.program_id(2)
    ts = h_ref.shape[0]
    nchunk = ts // CHUNK

    @pl.when(j == 0)
    def _ssm_input_and_b_gate():
        xn_ref[...] = _rms(h_ref[...], g_ref[...]).astype(BF16)
        xn = xn_ref[...]
        _store_planes(stage_ref, _dot(xn, w_ref[:, :SSM_WIDTH]))
        bg_ref[...] = _dot(xn, w_ref[:, SSM_WIDTH:])
        for p in range(PLANES):
            for l in range(CHUNK):
                piece = stage_ref[p, pl.ds(l, nchunk, stride=CHUNK), :]
                v_ref[p, :, l * LANES:(l + 1) * LANES] = piece.astype(v_ref.dtype)

    @pl.when(jnp.logical_and(j == 1, s == 0))
    def _():
        tail_ref[...] = jnp.zeros_like(tail_ref)

    @pl.when(j == 1)
    def _conv():
        xn = xn_ref[...]
        row = lax.broadcasted_iota(jnp.int32, (ts, 1), 0)
        for c in range(CONV_WIDTH // cblk):
            cs = slice(c * cblk, (c + 1) * cblk)
            vs = slice(CONV_WIDTH + c * cblk, CONV_WIDTH + (c + 1) * cblk)
            z = _dot(xn, w_ref[:, cs]) * _dot(xn, w_ref[:, vs])
            prev1 = tail_ref[SUBLANES - 1:SUBLANES, cs]
            prev2 = tail_ref[SUBLANES - 2:SUBLANES - 1, cs]
            z1 = jnp.where(row == 0, prev1, pltpu.roll(z, 1, axis=0))
            z2 = jnp.where(row == 0, prev2, jnp.where(row == 1, prev1, pltpu.roll(z, 2, axis=0)))
            conv = cb_ref[:, cs] + cw_ref[0:1, cs] * z2 + cw_ref[1:2, cs] * z1 + cw_ref[2:3, cs] * z
            yb_ref[:, cs] = (bg_ref[:, cs] * conv).astype(yb_ref.dtype)
            tail_ref[:, cs] = z[ts - SUBLANES:, :]

    @pl.when(j >= GATE_GROUP0)
    def _gates():
        gate_ref[...] = jax.nn.sigmoid(_dot(xn_ref[...], w_ref[...])).astype(gate_ref.dtype)


def _in_proj(h, g, w, conv_w, conv_b, *, batch, seq, ts=512, cblk=256):
    t, d = h.shape
    nseq = seq // ts
    gw = 2 * CONV_WIDTH
    ngroups = w.shape[1] // gw
    tok = lambda b, s, j: (b * nseq + s, 0)
    const = lambda b, s, j: (0, 0)
    return pl.pallas_call(
        functools.partial(_in_proj_kernel, cblk=cblk),
        grid=(batch, nseq, ngroups),
        in_specs=[pl.BlockSpec((ts, d), tok),
                  pl.BlockSpec((1, d), const),
                  pl.BlockSpec((d, gw), lambda b, s, j: (0, j)),
                  pl.BlockSpec((CONV_K, CONV_WIDTH), const),
                  pl.BlockSpec((1, CONV_WIDTH), const)],
        out_specs=(pl.BlockSpec((PLANES, ts // CHUNK, CHUNK * LANES),
                                lambda b, s, j: (0, b * nseq + s, 0)),
                   pl.BlockSpec((ts, CONV_WIDTH), tok),
                   pl.BlockSpec((ts, gw),
                                lambda b, s, j: (b * nseq + s, jnp.maximum(j - GATE_GROUP0, 0)))),
        out_shape=(jax.ShapeDtypeStruct((PLANES, t // CHUNK, CHUNK * LANES), BF16),
                   jax.ShapeDtypeStruct((t, CONV_WIDTH), BF16),
                   jax.ShapeDtypeStruct((t, 2 * d), BF16)),
        scratch_shapes=[pltpu.VMEM((ts, d), BF16),
                        pltpu.VMEM((ts, CONV_WIDTH), F32),
                        pltpu.VMEM((PLANES, ts, LANES), F32),
                        pltpu.VMEM((SUBLANES, CONV_WIDTH), F32)],
        compiler_params=pltpu.CompilerParams(
            dimension_semantics=("arbitrary", "arbitrary", "arbitrary"),
            vmem_limit_bytes=VMEM_LIMIT),
        name="in_proj",
    )(h, g, w, conv_w, conv_b)


def _block_diag_tile(src, period, rows_per_group, cols_per_group):
    nrows = src.shape[0]
    ncols = cols_per_group * GROUPS_PER_PLANE
    k = lax.broadcasted_iota(jnp.int32, (src.shape[1], ncols), 0)
    c = lax.broadcasted_iota(jnp.int32, (src.shape[1], ncols), 1)
    tiled = _dot_f32(src, (k == c % period).astype(F32))
    r = lax.broadcasted_iota(jnp.int32, (nrows, ncols), 0)
    c = lax.broadcasted_iota(jnp.int32, (nrows, ncols), 1)
    return jnp.where(r // rows_per_group == c // cols_per_group, tiled, 0.0)


def _ssm_kernel(x_ref, btr_ref, bti_ref, ctr_ref, cti_ref, arow_ref, acol_ref, d_ref, o_ref,
                toep_ref, win_ref, wout_ref, *, chunks):
    half = HALF_STATE

    @pl.when(pl.program_id(1) == 0)
    def _build_operators():
        btr = _block_diag_tile(btr_ref[...], SSM_STATE, SSM_GROUP, SSM_STATE)
        bti = _block_diag_tile(bti_ref[...], SSM_STATE, SSM_GROUP, SSM_STATE)
        ctr = _block_diag_tile(ctr_ref[...], SSM_GROUP, SSM_STATE, SSM_GROUP)
        cti = _block_diag_tile(cti_ref[...], SSM_GROUP, SSM_STATE, SSM_GROUP)
        zero = jnp.zeros((LANES, LANES), BF16)
        for k in range(CHUNK):
            ar = arow_ref[k:k + 1, :half]
            ai = arow_ref[k:k + 1, half:]
            wr = btr * ar - bti * ai
            wi = btr * ai + bti * ar
            rows = slice((CHUNK - 1 - k) * LANES, (CHUNK - k) * LANES)
            win_ref[rows, :half] = wr.astype(BF16)
            win_ref[rows, half:] = wi.astype(BF16)
            lag = (_dot_f32(wr, ctr) - _dot_f32(wi, cti)).astype(BF16)
            for lin in range(CHUNK - k):
                lout = lin + k
                toep_ref[lin * LANES:(lin + 1) * LANES, lout * LANES:(lout + 1) * LANES] = lag
        for lin in range(1, CHUNK, MXU_DIM // LANES):
            toep_ref[lin * LANES:(lin + 1) * LANES, (lin - 1) * LANES:lin * LANES] = zero
        for l in range(CHUNK):
            arc = acol_ref[:, l + 1:l + 2]
            aic = acol_ref[:, ACOL_IM + l + 1:ACOL_IM + l + 2]
            cols = slice(l * LANES, (l + 1) * LANES)
            wout_ref[:half, cols] = (ctr * arc - cti * aic).astype(BF16)
            wout_ref[half:, cols] = (-(ctr * aic + cti * arc)).astype(BF16)

    rows, width = x_ref.shape
    s_in = _dot(x_ref[...], win_ref[...])
    sr = s_in[:, :half]
    si = s_in[:, half:]
    cidx = lax.broadcasted_iota(jnp.int32, (rows, 1), 0) % chunks
    for k in range(SCAN_STEPS):
        sh = 1 << k
        ar = arow_ref[SCAN_ROW0 + k:SCAN_ROW0 + k + 1, :half]
        ai = arow_ref[SCAN_ROW0 + k:SCAN_ROW0 + k + 1, half:]
        keep = cidx >= sh
        pr = jnp.where(keep, pltpu.roll(sr, sh, axis=0), 0.0)
        pi = jnp.where(keep, pltpu.roll(si, sh, axis=0), 0.0)
        sr, si = sr + (ar * pr - ai * pi), si + (ar * pi + ai * pr)
    first = cidx >= 1
    pr = jnp.where(first, pltpu.roll(sr, 1, axis=0), 0.0).astype(BF16)
    pi = jnp.where(first, pltpu.roll(si, 1, axis=0), 0.0).astype(BF16)
    s_prev = jnp.concatenate([pr, pi], axis=1)
    for cb in range(width // MXU_DIM):
        cols = slice(cb * MXU_DIM, (cb + 1) * MXU_DIM)
        kk = (cb + 1) * MXU_DIM
        y = _dot(x_ref[:, :kk], toep_ref[:kk, cols]) + _dot(s_prev, wout_ref[:, cols])
        y = y + d_ref[:, cols] * x_ref[:, cols].astype(F32)
        o_ref[:, cols] = jax.nn.gelu(y).astype(o_ref.dtype)


def _ssm(xc, btr, bti, ctr, cti, arow, acol, d_tiled, *, chunks, rows=512):
    planes, nrows, width = xc.shape
    nh = nrows // rows

    def plane(*shape):
        return pl.BlockSpec((None,) + shape, lambda j, h: (j,) + (0,) * len(shape))

    return pl.pallas_call(
        functools.partial(_ssm_kernel, chunks=chunks),
        grid=(planes, nh),
        in_specs=[
            pl.BlockSpec((None, rows, width), lambda j, h: (j, h, 0)),
            plane(LANES, LANES), plane(LANES, LANES),
            plane(HALF_STATE, LANES), plane(HALF_STATE, LANES),
            plane(APOW_ROWS, PLANE_STATE), plane(HALF_STATE, LANES),
            plane(1, width),
        ],
        out_specs=pl.BlockSpec((None, rows, width), lambda j, h: (j, h, 0)),
        out_shape=jax.ShapeDtypeStruct((planes, nrows, width), BF16),
        scratch_shapes=[pltpu.VMEM((width, width), BF16),
                        pltpu.VMEM((width, PLANE_STATE), BF16),
                        pltpu.VMEM((PLANE_STATE, width), BF16)],
        compiler_params=pltpu.CompilerParams(
            dimension_semantics=("arbitrary", "arbitrary"), vmem_limit_bytes=VMEM_LIMIT),
        name="ssm_scan",
    )(xc, btr, bti, ctr, cti, arow, acol, d_tiled)


def _ssm_tables(lam_re, lam_im, log_dt, b_re, b_im, c_re, c_im, d_skip):
    lam_re = jnp.minimum(lam_re, -1e-4)
    dt = jnp.exp(log_dt)[:, None]
    mag = jnp.exp(lam_re * dt)
    a_re = mag * jnp.cos(lam_im * dt)
    a_im = mag * jnp.sin(lam_im * dt)
    den = lam_re * lam_re + lam_im * lam_im
    p = a_re - 1.0
    f_re = ((p * lam_re + a_im * lam_im) / den)[:, :, None]
    f_im = ((a_im * lam_re - p * lam_im) / den)[:, :, None]
    bb_re = f_re * b_re - f_im * b_im
    bb_im = f_re * b_im + f_im * b_re

    N, C, P, L = SSM_STATE, SSM_GROUP, PLANES, CHUNK

    def bt(bb):
        m = jnp.transpose(bb, (0, 2, 1)).reshape(P, LANES, N)
        return jnp.pad(m, ((0, 0), (0, 0), (0, LANES - N)))

    def ct(c):
        m = jnp.transpose(c, (0, 2, 1)).reshape(P, HALF_STATE, C)
        return jnp.pad(m, ((0, 0), (0, 0), (0, LANES - C)))

    ks = jnp.concatenate([jnp.arange(L + 1), L * (2 ** jnp.arange(SCAN_STEPS))]).astype(F32)
    ks = ks.reshape(-1, 1, 1)
    m = jnp.exp(ks * (lam_re * dt))
    pr = m * jnp.cos(ks * (lam_im * dt))
    pi = m * jnp.sin(ks * (lam_im * dt))
    arow = jnp.concatenate([pr.reshape(APOW_ROWS, P, HALF_STATE),
                            pi.reshape(APOW_ROWS, P, HALF_STATE)], axis=2)
    arow = jnp.transpose(arow, (1, 0, 2))

    def col(pw):
        c = jnp.transpose(pw[:L + 1].reshape(L + 1, P, HALF_STATE), (1, 2, 0))
        return jnp.pad(c, ((0, 0), (0, 0), (0, ACOL_IM - (L + 1))))

    acol = jnp.concatenate([col(pr), col(pi)], axis=2)
    d_tiled = jnp.tile(d_skip.reshape(P, 1, LANES), (1, 1, L))
    return bt(bb_re), bt(bb_im), ct(c_re), ct(c_im), arow, acol, d_tiled


def _mix_out_kernel(g_ref, yb_ref, gate_ref, h_ref, wglu_ref, bglu_ref, wa_ref, wb_ref, wo_ref, o_ref,
                    gs_ref):
    nchunk = g_ref.shape[1]
    for p in range(PLANES):
        for l in range(CHUNK):
            piece = g_ref[p, :, l * LANES:(l + 1) * LANES].astype(F32)
            gs_ref[p, pl.ds(l, nchunk, stride=CHUNK), :] = piece
    g = _load_planes(gs_ref, 0, PLANES)
    glu = _dot(g.astype(BF16), wglu_ref[...]) + bglu_ref[...]
    y_a = (g * jax.nn.sigmoid(glu)).astype(BF16)
    z_a = _dot(y_a, wa_ref[...])
    z_b = _dot(yb_ref[...], wb_ref[...])
    merged = (gate_ref[:, :D_MODEL].astype(F32) * z_a
              + gate_ref[:, D_MODEL:].astype(F32) * z_b).astype(BF16)
    o_ref[...] = h_ref[...] + _dot(merged, wo_ref[...])


def _mix_out(g_planes, y_b, gates, h, w_glu, b_glu, w_a, w_b, w_o, *, tm=512):
    t, d = h.shape

    def const(shape):
        return pl.BlockSpec(shape, lambda i: (0,) * len(shape), pipeline_mode=pl.Buffered(1))

    return pl.pallas_call(
        _mix_out_kernel,
        grid=(t // tm,),
        in_specs=[
            pl.BlockSpec((PLANES, tm // CHUNK, CHUNK * LANES), lambda i: (0, i, 0)),
            pl.BlockSpec((tm, CONV_WIDTH), lambda i: (i, 0)),
            pl.BlockSpec((tm, 2 * d), lambda i: (i, 0)),
            pl.BlockSpec((tm, d), lambda i: (i, 0)),
            const((SSM_WIDTH, SSM_WIDTH)), const((1, SSM_WIDTH)),
            const((SSM_WIDTH, d)), const((CONV_WIDTH, d)), const((d, d)),
        ],
        out_specs=pl.BlockSpec((tm, d), lambda i: (i, 0)),
        out_shape=jax.ShapeDtypeStruct((t, d), F32),
        scratch_shapes=[pltpu.VMEM((PLANES, tm, LANES), F32)],
        compiler_params=pltpu.CompilerParams(
            dimension_semantics=("parallel",), vmem_limit_bytes=VMEM_LIMIT),
        name="mix_out",
    )(g_planes, y_b, gates, h, w_glu, b_glu, w_a, w_b, w_o)


def kernel(x, ffn1_norm, ffn1_w_gate, ffn1_w_up, ffn1_w_down, mix_norm, w_in, ssm_lambda_re, ssm_lambda_im, ssm_log_dt, ssm_b_re, ssm_b_im, ssm_c_re, ssm_c_im, ssm_d, ssm_w_glu, ssm_b_glu, ssm_w_out, conv_w, conv_b, conv_w_out, w_o, ffn2_norm, ffn2_w_gate, ffn2_w_up, ffn2_w_down, final_norm):
    batch, seq, d = x.shape
    t = batch * seq
    chunks = seq // CHUNK

    bf = lambda w: w.astype(BF16)
    vec = lambda g: g.reshape(1, -1).astype(F32)

    later = (ffn2_w_gate, ffn2_w_up, ffn2_w_down, w_in, ssm_w_glu, ssm_w_out, conv_w_out, w_o)
    h1, wg2, wu2, wd2, w_in_b, w_glu_b, w_a_b, w_b_b, w_o_b = _ffn(
        x.reshape(t, d), vec(ffn1_norm), bf(ffn1_w_gate), bf(ffn1_w_up), bf(ffn1_w_down),
        cast=later)

    v_chunks, y_b, gates = _in_proj(h1, vec(mix_norm), w_in_b, conv_w.astype(F32), vec(conv_b),
                                    batch=batch, seq=seq)

    tables = _ssm_tables(ssm_lambda_re, ssm_lambda_im, ssm_log_dt, ssm_b_re, ssm_b_im,
                         ssm_c_re, ssm_c_im, ssm_d)
    g_planes = _ssm(v_chunks, *tables, chunks=chunks)

    h2 = _mix_out(g_planes, y_b, gates, h1, w_glu_b, vec(ssm_b_glu), w_a_b, w_b_b, w_o_b)

    out = _ffn(h2, vec(ffn2_norm), wg2, wu2, wd2, vec(final_norm))
    return out.reshape(batch, seq, d)
```

```python
import functools

import jax
import jax.numpy as jnp
from jax import lax
from jax.experimental import pallas as pl
from jax.experimental.pallas import tpu as pltpu

D_MODEL = 2048
SSM_WIDTH = 1024
SSM_GROUP = 16
SSM_GROUPS = 64
SSM_STATE = 64
CONV_WIDTH = 1024
CONV_K = 3
EPS = 1e-6

LANES = 128
SUBLANES = 8
BF16_SUBLANES = 16
MXU_DIM = 256
CHUNK = 16
PLANES = SSM_WIDTH // LANES
GROUPS_PER_PLANE = LANES // SSM_GROUP
HALF_STATE = GROUPS_PER_PLANE * SSM_STATE
PLANE_STATE = 2 * HALF_STATE
SCAN_STEPS = 7
SCAN_ROW0 = CHUNK + 1
APOW_ROWS = SCAN_ROW0 + SCAN_STEPS
ACOL_IM = LANES // 2
VMEM_BYTES = 64 * 1024 * 1024
VMEM_LIMIT = VMEM_BYTES - 8 * 1024 * 1024
VMEM_LIMIT_FFN = VMEM_BYTES - 4 * 1024 * 1024
VMEM_LIMIT_IN_PROJ = VMEM_BYTES - 2 * 1024 * 1024

BF16 = jnp.bfloat16
F32 = jnp.float32
HIGHEST = lax.Precision.HIGHEST


def _rms(x, g):
    return x * lax.rsqrt(jnp.mean(x * x, axis=-1, keepdims=True) + EPS) * g


def _dot(a, b):
    return jnp.dot(a, b, preferred_element_type=F32)


def _dot_f32(a, b):
    return jnp.dot(a, b, precision=HIGHEST, preferred_element_type=F32)


def _ffn_kernel(x_ref, gin_ref, wg_ref, wu_ref, wd_ref, *refs, final_norm, d_ff, n_cast):
    if final_norm:
        gout_ref, *refs = refs
    cast_src = refs[:n_cast]
    o_ref = refs[n_cast]
    cast_dst = refs[n_cast + 1:2 * n_cast + 1]
    xn_ref = refs[2 * n_cast + 1]
    f = pl.program_id(1)
    nf = pl.num_programs(1)
    tf = wd_ref.shape[0]

    @pl.when(f == 0)
    def _():
        x = x_ref[...]
        xn_ref[...] = _rms(x, gin_ref[...]).astype(BF16)
        o_ref[...] = x

    def accumulate(width):
        xn = xn_ref[...]
        gate = _dot(xn, wg_ref[:, :width])
        up = _dot(xn, wu_ref[:, :width])
        act = ((0.5 * gate) * jax.nn.sigmoid(gate) * up).astype(BF16)
        o_ref[...] += _dot(act, wd_ref[:width, :])

    last_width = d_ff - (pl.cdiv(d_ff, tf) - 1) * tf
    if last_width == tf:
        accumulate(tf)
    else:
        pl.when(f < nf - 1)(functools.partial(accumulate, tf))
        pl.when(f == nf - 1)(functools.partial(accumulate, last_width))

    if final_norm:
        @pl.when(f == nf - 1)
        def _():
            o_ref[...] = _rms(o_ref[...], gout_ref[...])

    for src, dst in zip(cast_src, cast_dst):
        dst[...] = src[...].astype(dst.dtype)


def _cast_block_rows(nrows, steps):
    br = BF16_SUBLANES
    while nrows % br or nrows // br > steps:
        br += BF16_SUBLANES
    return br


def _ffn(x, g_in, wg, wu, wd, g_out=None, *, cast=(), tm=1024, tf=512):
    t, d = x.shape
    d_ff = wg.shape[1]
    final_norm = g_out is not None
    nt = t // tm
    nf = pl.cdiv(d_ff, tf)
    steps = nt * nf
    row = pl.BlockSpec((tm, d), lambda i, f: (i, 0))
    vec = pl.BlockSpec((1, d), lambda i, f: (0, 0))
    in_specs = [row, vec,
                pl.BlockSpec((d, tf), lambda i, f: (0, f)),
                pl.BlockSpec((d, tf), lambda i, f: (0, f)),
                pl.BlockSpec((tf, d), lambda i, f: (f, 0))]
    args = [x, g_in, wg, wu, wd]
    if final_norm:
        in_specs.append(vec)
        args.append(g_out)
    def cast_specs():
        specs = []
        for w in cast:
            nrows, ncols = w.shape
            br = _cast_block_rows(nrows, steps)
            specs.append(pl.BlockSpec(
                (br, ncols), lambda i, f, nb=nrows // br: (jnp.minimum(i * nf + f, nb - 1), 0)))
        return specs

    outs = pl.pallas_call(
        functools.partial(_ffn_kernel, final_norm=final_norm, d_ff=d_ff, n_cast=len(cast)),
        grid=(nt, nf),
        in_specs=in_specs + cast_specs(),
        out_specs=[row] + cast_specs(),
        out_shape=[jax.ShapeDtypeStruct((t, d), F32)]
        + [jax.ShapeDtypeStruct(w.shape, BF16) for w in cast],
        scratch_shapes=[pltpu.VMEM((tm, d), BF16)],
        compiler_params=pltpu.CompilerParams(
            dimension_semantics=("arbitrary", "arbitrary"),
            vmem_limit_bytes=VMEM_LIMIT_FFN),
        name="ffn_final" if final_norm else "ffn_hidden",
    )(*args, *cast)
    return outs if cast else outs[0]


GATE_GROUP0 = 2


def _store_planes(ref, val):
    for p in range(ref.shape[0]):
        ref[p] = val[:, p * LANES:(p + 1) * LANES]


def _load_planes(ref, p0, p1):
    return jnp.concatenate([ref[p] for p in range(p0, p1)], axis=1)


def _in_proj_kernel(h_ref, g_ref, w_ref, cw_ref, cb_ref, v_ref, yb_ref, gate_ref,
                    xn_ref, planes_ref, tail_ref, *, cblk):
    s = pl.program_id(1)
    j = pl.program_id(2)
    ts = h_ref.shape[0]
    nchunk = ts // CHUNK

    @pl.when(j == 0)
    def _ssm_input_and_b_gate():
        xn_ref[...] = _rms(h_ref[...], g_ref[...]).astype(BF16)
        xn = xn_ref[...]
        _store_planes(planes_ref, _dot(xn, w_ref[:, :SSM_WIDTH]))
        for p in range(PLANES):
            for l in range(CHUNK):
                piece = planes_ref[p, pl.ds(l, nchunk, stride=CHUNK), :]
                v_ref[p, :, l * LANES:(l + 1) * LANES] = piece.astype(v_ref.dtype)
        _store_planes(planes_ref, _dot(xn, w_ref[:, SSM_WIDTH:]))

    @pl.when(jnp.logical_and(j == 1, s == 0))
    def _():
        tail_ref[...] = jnp.zeros_like(tail_ref)

    @pl.when(j == 1)
    def _conv():
        xn = xn_ref[...]
        row = lax.broadcasted_iota(jnp.int32, (ts, 1), 0)
        for c in range(CONV_WIDTH // cblk):
            cs = slice(c * cblk, (c + 1) * cblk)
            vs = slice(CONV_WIDTH + c * cblk, CONV_WIDTH + (c + 1) * cblk)
            z = _dot(xn, w_ref[:, cs]) * _dot(xn, w_ref[:, vs])
            prev1 = tail_ref[SUBLANES - 1:SUBLANES, cs]
            prev2 = tail_ref[SUBLANES - 2:SUBLANES - 1, cs]
            z1 = jnp.where(row == 0, prev1, pltpu.roll(z, 1, axis=0))
            z2 = jnp.where(row == 0, prev2, jnp.where(row == 1, prev1, pltpu.roll(z, 2, axis=0)))
            conv = cb_ref[:, cs] + cw_ref[0:1, cs] * z2 + cw_ref[1:2, cs] * z1 + cw_ref[2:3, cs] * z
            b_gate = _load_planes(planes_ref, c * cblk // LANES, (c + 1) * cblk // LANES)
            yb_ref[:, cs] = (b_gate * conv).astype(yb_ref.dtype)
            tail_ref[:, cs] = z[ts - SUBLANES:, :]

    @pl.when(j >= GATE_GROUP0)
    def _gates():
        gate_ref[...] = jax.nn.sigmoid(_dot(xn_ref[...], w_ref[...])).astype(gate_ref.dtype)


def _in_proj(h, g, w, conv_w, conv_b, *, batch, seq, ts=1024, cblk=256):
    t, d = h.shape
    nseq = seq // ts
    gw = 2 * CONV_WIDTH
    ngroups = w.shape[1] // gw
    tok = lambda b, s, j: (b * nseq + s, 0)
    const = lambda b, s, j: (0, 0)
    return pl.pallas_call(
        functools.partial(_in_proj_kernel, cblk=cblk),
        grid=(batch, nseq, ngroups),
        in_specs=[pl.BlockSpec((ts, d), tok),
                  pl.BlockSpec((1, d), const),
                  pl.BlockSpec((d, gw), lambda b, s, j: (0, j)),
                  pl.BlockSpec((CONV_K, CONV_WIDTH), const),
                  pl.BlockSpec((1, CONV_WIDTH), const)],
        out_specs=(pl.BlockSpec((PLANES, ts // CHUNK, CHUNK * LANES),
                                lambda b, s, j: (0, b * nseq + s, 0)),
                   pl.BlockSpec((ts, CONV_WIDTH), tok),
                   pl.BlockSpec((ts, gw),
                                lambda b, s, j: (b * nseq + s, jnp.maximum(j - GATE_GROUP0, 0)))),
        out_shape=(jax.ShapeDtypeStruct((PLANES, t // CHUNK, CHUNK * LANES), BF16),
                   jax.ShapeDtypeStruct((t, CONV_WIDTH), BF16),
                   jax.ShapeDtypeStruct((t, 2 * d), BF16)),
        scratch_shapes=[pltpu.VMEM((ts, d), BF16),
                        pltpu.VMEM((PLANES, ts, LANES), F32),
                        pltpu.VMEM((SUBLANES, CONV_WIDTH), F32)],
        compiler_params=pltpu.CompilerParams(
            dimension_semantics=("arbitrary", "arbitrary", "arbitrary"),
            vmem_limit_bytes=VMEM_LIMIT_IN_PROJ),
        name="in_proj",
    )(h, g, w, conv_w, conv_b)


def _block_diag_tile(src, period, rows_per_group, cols_per_group):
    nrows = src.shape[0]
    ncols = cols_per_group * GROUPS_PER_PLANE
    k = lax.broadcasted_iota(jnp.int32, (src.shape[1], ncols), 0)
    c = lax.broadcasted_iota(jnp.int32, (src.shape[1], ncols), 1)
    tiled = _dot_f32(src, (k == c % period).astype(F32))
    r = lax.broadcasted_iota(jnp.int32, (nrows, ncols), 0)
    c = lax.broadcasted_iota(jnp.int32, (nrows, ncols), 1)
    return jnp.where(r // rows_per_group == c // cols_per_group, tiled, 0.0)


def _ssm_kernel(x_ref, btr_ref, bti_ref, ctr_ref, cti_ref, arow_ref, acol_ref, d_ref, o_ref,
                toep_ref, win_ref, wout_ref, *, chunks):
    half = HALF_STATE

    @pl.when(pl.program_id(1) == 0)
    def _build_operators():
        btr = _block_diag_tile(btr_ref[...], SSM_STATE, SSM_GROUP, SSM_STATE)
        bti = _block_diag_tile(bti_ref[...], SSM_STATE, SSM_GROUP, SSM_STATE)
        ctr = _block_diag_tile(ctr_ref[...], SSM_GROUP, SSM_STATE, SSM_GROUP)
        cti = _block_diag_tile(cti_ref[...], SSM_GROUP, SSM_STATE, SSM_GROUP)
        zero = jnp.zeros((LANES, LANES), BF16)
        for k in range(CHUNK):
            ar = arow_ref[k:k + 1, :half]
            ai = arow_ref[k:k + 1, half:]
            wr = btr * ar - bti * ai
            wi = btr * ai + bti * ar
            rows = slice((CHUNK - 1 - k) * LANES, (CHUNK - k) * LANES)
            win_ref[rows, :half] = wr.astype(BF16)
            win_ref[rows, half:] = wi.astype(BF16)
            lag = (_dot_f32(wr, ctr) - _dot_f32(wi, cti)).astype(BF16)
            for lin in range(CHUNK - k):
                lout = lin + k
                toep_ref[lin * LANES:(lin + 1) * LANES, lout * LANES:(lout + 1) * LANES] = lag
        for lin in range(1, CHUNK, MXU_DIM // LANES):
            toep_ref[lin * LANES:(lin + 1) * LANES, (lin - 1) * LANES:lin * LANES] = zero
        for l in range(CHUNK):
            arc = acol_ref[:, l + 1:l + 2]
            aic = acol_ref[:, ACOL_IM + l + 1:ACOL_IM + l + 2]
            cols = slice(l * LANES, (l + 1) * LANES)
            wout_ref[:half, cols] = (ctr * arc - cti * aic).astype(BF16)
            wout_ref[half:, cols] = (-(ctr * aic + cti * arc)).astype(BF16)

    rows, width = x_ref.shape
    s_in = _dot(x_ref[...], win_ref[...])
    sr = s_in[:, :half]
    si = s_in[:, half:]
    cidx = lax.broadcasted_iota(jnp.int32, (rows, 1), 0) % chunks
    for k in range(SCAN_STEPS):
        sh = 1 << k
        ar = arow_ref[SCAN_ROW0 + k:SCAN_ROW0 + k + 1, :half]
        ai = arow_ref[SCAN_ROW0 + k:SCAN_ROW0 + k + 1, half:]
        keep = cidx >= sh
        pr = jnp.where(keep, pltpu.roll(sr, sh, axis=0), 0.0)
        pi = jnp.where(keep, pltpu.roll(si, sh, axis=0), 0.0)
        sr, si = sr + (ar * pr - ai * pi), si + (ar * pi + ai * pr)
    first = cidx >= 1
    pr = jnp.where(first, pltpu.roll(sr, 1, axis=0), 0.0).astype(BF16)
    pi = jnp.where(first, pltpu.roll(si, 1, axis=0), 0.0).astype(BF16)
    s_prev = jnp.concatenate([pr, pi], axis=1)
    for cb in range(width // MXU_DIM):
        cols = slice(cb * MXU_DIM, (cb + 1) * MXU_DIM)
        kk = (cb + 1) * MXU_DIM
        y = _dot(x_ref[:, :kk], toep_ref[:kk, cols]) + _dot(s_prev, wout_ref[:, cols])
        y = y + d_ref[:, cols] * x_ref[:, cols].astype(F32)
        o_ref[:, cols] = jax.nn.gelu(y).astype(o_ref.dtype)


def _ssm(xc, btr, bti, ctr, cti, arow, acol, d_tiled, *, chunks, rows=512):
    planes, nrows, width = xc.shape
    nh = nrows // rows

    def plane(*shape):
        return pl.BlockSpec((None,) + shape, lambda j, h: (j,) + (0,) * len(shape))

    return pl.pallas_call(
        functools.partial(_ssm_kernel, chunks=chunks),
        grid=(planes, nh),
        in_specs=[
            pl.BlockSpec((None, rows, width), lambda j, h: (j, h, 0)),
            plane(LANES, LANES), plane(LANES, LANES),
            plane(HALF_STATE, LANES), plane(HALF_STATE, LANES),
            plane(APOW_ROWS, PLANE_STATE), plane(HALF_STATE, LANES),
            plane(1, width),
        ],
        out_specs=pl.BlockSpec((None, rows, width), lambda j, h: (j, h, 0)),
        out_shape=jax.ShapeDtypeStruct((planes, nrows, width), BF16),
        scratch_shapes=[pltpu.VMEM((width, width), BF16),
                        pltpu.VMEM((width, PLANE_STATE), BF16),
                        pltpu.VMEM((PLANE_STATE, width), BF16)],
        compiler_params=pltpu.CompilerParams(
            dimension_semantics=("arbitrary", "arbitrary"), vmem_limit_bytes=VMEM_LIMIT),
        name="ssm_scan",
    )(xc, btr, bti, ctr, cti, arow, acol, d_tiled)


def _ssm_tables(lam_re, lam_im, log_dt, b_re, b_im, c_re, c_im, d_skip):
    lam_re = jnp.minimum(lam_re, -1e-4)
    dt = jnp.exp(log_dt)[:, None]
    mag = jnp.exp(lam_re * dt)
    a_re = mag * jnp.cos(lam_im * dt)
    a_im = mag * jnp.sin(lam_im * dt)
    den = lam_re * lam_re + lam_im * lam_im
    p = a_re - 1.0
    f_re = ((p * lam_re + a_im * lam_im) / den)[:, :, None]
    f_im = ((a_im * lam_re - p * lam_im) / den)[:, :, None]
    bb_re = f_re * b_re - f_im * b_im
    bb_im = f_re * b_im + f_im * b_re

    N, C, P, L = SSM_STATE, SSM_GROUP, PLANES, CHUNK

    def bt(bb):
        m = jnp.transpose(bb, (0, 2, 1)).reshape(P, LANES, N)
        return jnp.pad(m, ((0, 0), (0, 0), (0, LANES - N)))

    def ct(c):
        m = jnp.transpose(c, (0, 2, 1)).reshape(P, HALF_STATE, C)
        return jnp.pad(m, ((0, 0), (0, 0), (0, LANES - C)))

    ks = jnp.concatenate([jnp.arange(L + 1), L * (2 ** jnp.arange(SCAN_STEPS))]).astype(F32)
    ks = ks.reshape(-1, 1, 1)
    m = jnp.exp(ks * (lam_re * dt))
    pr = m * jnp.cos(ks * (lam_im * dt))
    pi = m * jnp.sin(ks * (lam_im * dt))
    arow = jnp.concatenate([pr.reshape(APOW_ROWS, P, HALF_STATE),
                            pi.reshape(APOW_ROWS, P, HALF_STATE)], axis=2)
    arow = jnp.transpose(arow, (1, 0, 2))

    def col(pw):
        c = jnp.transpose(pw[:L + 1].reshape(L + 1, P, HALF_STATE), (1, 2, 0))
        return jnp.pad(c, ((0, 0), (0, 0), (0, ACOL_IM - (L + 1))))

    acol = jnp.concatenate([col(pr), col(pi)], axis=2)
    d_tiled = jnp.tile(d_skip.reshape(P, 1, LANES), (1, 1, L))
    return bt(bb_re), bt(bb_im), ct(c_re), ct(c_im), arow, acol, d_tiled


def _mix_out_kernel(g_ref, yb_ref, gate_ref, h_ref, wglu_ref, bglu_ref, wa_ref, wb_ref, wo_ref, o_ref,
                    gs_ref):
    nchunk = g_ref.shape[1]
    for p in range(PLANES):
        for l in range(CHUNK):
            piece = g_ref[p, :, l * LANES:(l + 1) * LANES].astype(F32)
            gs_ref[p, pl.ds(l, nchunk, stride=CHUNK), :] = piece
    g = _load_planes(gs_ref, 0, PLANES)
    glu = _dot(g.astype(BF16), wglu_ref[...]) + bglu_ref[...]
    y_a = (g * jax.nn.sigmoid(glu)).astype(BF16)
    z_a = _dot(y_a, wa_ref[...])
    z_b = _dot(yb_ref[...], wb_ref[...])
    merged = (gate_ref[:, :D_MODEL].astype(F32) * z_a
              + gate_ref[:, D_MODEL:].astype(F32) * z_b).astype(BF16)
    o_ref[...] = h_ref[...] + _dot(merged, wo_ref[...])


def _mix_out(g_planes, y_b, gates, h, w_glu, b_glu, w_a, w_b, w_o, *, tm=512):
    t, d = h.shape

    def const(shape):
        return pl.BlockSpec(shape, lambda i: (0,) * len(shape), pipeline_mode=pl.Buffered(1))

    return pl.pallas_call(
        _mix_out_kernel,
        grid=(t // tm,),
        in_specs=[
            pl.BlockSpec((PLANES, tm // CHUNK, CHUNK * LANES), lambda i: (0, i, 0)),
            pl.BlockSpec((tm, CONV_WIDTH), lambda i: (i, 0)),
            pl.BlockSpec((tm, 2 * d), lambda i: (i, 0)),
            pl.BlockSpec((tm, d), lambda i: (i, 0)),
            const((SSM_WIDTH, SSM_WIDTH)), const((1, SSM_WIDTH)),
            const((SSM_WIDTH, d)), const((CONV_WIDTH, d)), const((d, d)),
        ],
        out_specs=pl.BlockSpec((tm, d), lambda i: (i, 0)),
        out_shape=jax.ShapeDtypeStruct((t, d), F32),
        scratch_shapes=[pltpu.VMEM((PLANES, tm, LANES), F32)],
        compiler_params=pltpu.CompilerParams(
            dimension_semantics=("parallel",), vmem_limit_bytes=VMEM_LIMIT),
        name="mix_out",
    )(g_planes, y_b, gates, h, w_glu, b_glu, w_a, w_b, w_o)


def kernel(x, ffn1_norm, ffn1_w_gate, ffn1_w_up, ffn1_w_down, mix_norm, w_in, ssm_lambda_re, ssm_lambda_im, ssm_log_dt, ssm_b_re, ssm_b_im, ssm_c_re, ssm_c_im, ssm_d, ssm_w_glu, ssm_b_glu, ssm_w_out, conv_w, conv_b, conv_w_out, w_o, ffn2_norm, ffn2_w_gate, ffn2_w_up, ffn2_w_down, final_norm):
    batch, seq, d = x.shape
    t = batch * seq
    chunks = seq // CHUNK

    bf = lambda w: w.astype(BF16)
    vec = lambda g: g.reshape(1, -1).astype(F32)

    later = (ffn2_w_gate, ffn2_w_up, ffn2_w_down, w_in, ssm_w_glu, ssm_w_out, conv_w_out, w_o)
    h1, wg2, wu2, wd2, w_in_b, w_glu_b, w_a_b, w_b_b, w_o_b = _ffn(
        x.reshape(t, d), vec(ffn1_norm), bf(ffn1_w_gate), bf(ffn1_w_up), bf(ffn1_w_down),
        cast=later)

    v_chunks, y_b, gates = _in_proj(h1, vec(mix_norm), w_in_b, conv_w.astype(F32), vec(conv_b),
                                    batch=batch, seq=seq)

    tables = _ssm_tables(ssm_lambda_re, ssm_lambda_im, ssm_log_dt, ssm_b_re, ssm_b_im,
                         ssm_c_re, ssm_c_im, ssm_d)
    g_planes = _ssm(v_chunks, *tables, chunks=chunks)

    h2 = _mix_out(g_planes, y_b, gates, h1, w_glu_b, vec(ssm_b_glu), w_a_b, w_b_b, w_o_b)

    out = _ffn(h2, vec(ffn2_norm), wg2, wu2, wd2, vec(final_norm))
    return out.reshape(batch, seq, d)
```

```python
import functools

import jax
import jax.numpy as jnp
from jax import lax
from jax.experimental import pallas as pl
from jax.experimental.pallas import tpu as pltpu

D_MODEL = 2048
SSM_WIDTH = 1024
SSM_GROUP = 16
SSM_GROUPS = 64
SSM_STATE = 64
CONV_WIDTH = 1024
CONV_K = 3
EPS = 1e-6

LANES = 128
SUBLANES = 8
BF16_SUBLANES = 16
MXU_DIM = 256
CHUNK = 16
PLANES = SSM_WIDTH // LANES
GROUPS_PER_PLANE = LANES // SSM_GROUP
HALF_STATE = GROUPS_PER_PLANE * SSM_STATE
PLANE_STATE = 2 * HALF_STATE
SCAN_STEPS_IN_GROUP = 3
SCAN_ROW0 = CHUNK + 1
CARRY_ROW0 = -(-(SCAN_ROW0 + SCAN_STEPS_IN_GROUP) // SUBLANES) * SUBLANES
APOW_ROWS = CARRY_ROW0 + SUBLANES
ACOL_IM = LANES // 2
VMEM_BYTES = 64 * 1024 * 1024
VMEM_LIMIT = VMEM_BYTES - 8 * 1024 * 1024
VMEM_LIMIT_FFN = VMEM_BYTES - 4 * 1024 * 1024
VMEM_LIMIT_IN_PROJ = VMEM_BYTES - 2 * 1024 * 1024

BF16 = jnp.bfloat16
F32 = jnp.float32
HIGHEST = lax.Precision.HIGHEST


def _rms(x, g):
    return x * lax.rsqrt(jnp.mean(x * x, axis=-1, keepdims=True) + EPS) * g


def _dot(a, b):
    return jnp.dot(a, b, preferred_element_type=F32)


def _dot_f32(a, b):
    return jnp.dot(a, b, precision=HIGHEST, preferred_element_type=F32)


def _split_bf16(a):
    hi = a.astype(BF16)
    return hi, (a - hi.astype(F32)).astype(BF16)


def _dot_split(a, b_parts):
    a_hi, a_lo = _split_bf16(a)
    b_hi, b_lo = b_parts
    return _dot(a_hi, b_hi) + (_dot(a_lo, b_hi) + _dot(a_hi, b_lo))


def _ffn_kernel(x_ref, gin_ref, wg_ref, wu_ref, wd_ref, *refs, final_norm, d_ff, n_cast):
    if final_norm:
        gout_ref, *refs = refs
    cast_src = refs[:n_cast]
    o_ref = refs[n_cast]
    cast_dst = refs[n_cast + 1:2 * n_cast + 1]
    xn_ref = refs[2 * n_cast + 1]
    f = pl.program_id(1)
    nf = pl.num_programs(1)
    tf = wd_ref.shape[0]

    @pl.when(f == 0)
    def _():
        x = x_ref[...]
        xn_ref[...] = _rms(x, gin_ref[...]).astype(BF16)
        o_ref[...] = x

    def accumulate(width):
        xn = xn_ref[...]
        gate = _dot(xn, wg_ref[:, :width])
        up = _dot(xn, wu_ref[:, :width])
        act = ((0.5 * gate) * jax.nn.sigmoid(gate) * up).astype(BF16)
        o_ref[...] += _dot(act, wd_ref[:width, :])

    last_width = d_ff - (pl.cdiv(d_ff, tf) - 1) * tf
    if last_width == tf:
        accumulate(tf)
    else:
        pl.when(f < nf - 1)(functools.partial(accumulate, tf))
        pl.when(f == nf - 1)(functools.partial(accumulate, last_width))

    if final_norm:
        @pl.when(f == nf - 1)
        def _():
            o_ref[...] = _rms(o_ref[...], gout_ref[...])

    for src, dst in zip(cast_src, cast_dst):
        dst[...] = src[...].astype(dst.dtype)


def _cast_block_rows(nrows, steps):
    br = BF16_SUBLANES
    while nrows % br or nrows // br > steps:
        br += BF16_SUBLANES
    return br


def _ffn(x, g_in, wg, wu, wd, g_out=None, *, cast=(), tm=1024, tf=512):
    t, d = x.shape
    d_ff = wg.shape[1]
    final_norm = g_out is not None
    nt = t // tm
    nf = pl.cdiv(d_ff, tf)
    steps = nt * nf
    row = pl.BlockSpec((tm, d), lambda i, f: (i, 0))
    vec = pl.BlockSpec((1, d), lambda i, f: (0, 0))
    in_specs = [row, vec,
                pl.BlockSpec((d, tf), lambda i, f: (0, f)),
                pl.BlockSpec((d, tf), lambda i, f: (0, f)),
                pl.BlockSpec((tf, d), lambda i, f: (f, 0))]
    args = [x, g_in, wg, wu, wd]
    if final_norm:
        in_specs.append(vec)
        args.append(g_out)
    def cast_specs():
        specs = []
        for w in cast:
            nrows, ncols = w.shape
            br = _cast_block_rows(nrows, steps)
            specs.append(pl.BlockSpec(
                (br, ncols), lambda i, f, nb=nrows // br: (jnp.minimum(i * nf + f, nb - 1), 0)))
        return specs

    outs = pl.pallas_call(
        functools.partial(_ffn_kernel, final_norm=final_norm, d_ff=d_ff, n_cast=len(cast)),
        grid=(nt, nf),
        in_specs=in_specs + cast_specs(),
        out_specs=[row] + cast_specs(),
        out_shape=[jax.ShapeDtypeStruct((t, d), F32)]
        + [jax.ShapeDtypeStruct(w.shape, BF16) for w in cast],
        scratch_shapes=[pltpu.VMEM((tm, d), BF16)],
        compiler_params=pltpu.CompilerParams(
            dimension_semantics=("arbitrary", "arbitrary"),
            vmem_limit_bytes=VMEM_LIMIT_FFN),
        name="ffn_final" if final_norm else "ffn_hidden",
    )(*args, *cast)
    return outs if cast else outs[0]


GATE_GROUP0 = 2


def _store_planes(ref, val):
    for p in range(ref.shape[0]):
        ref[p] = val[:, p * LANES:(p + 1) * LANES]


def _load_planes(ref, p0, p1):
    return jnp.concatenate([ref[p] for p in range(p0, p1)], axis=1)


def _in_proj_kernel(h_ref, g_ref, w_ref, cw_ref, cb_ref, v_ref, yb_ref, gate_ref,
                    xn_ref, planes_ref, tail_ref, *, cblk):
    s = pl.program_id(1)
    j = pl.program_id(2)
    ts = h_ref.shape[0]
    nchunk = ts // CHUNK

    @pl.when(j == 0)
    def _ssm_input_and_b_gate():
        xn_ref[...] = _rms(h_ref[...], g_ref[...]).astype(BF16)
        xn = xn_ref[...]
        _store_planes(planes_ref, _dot(xn, w_ref[:, :SSM_WIDTH]))
        for p in range(PLANES):
            for l in range(CHUNK):
                piece = planes_ref[p, pl.ds(l, nchunk, stride=CHUNK), :]
                v_ref[p, :, l * LANES:(l + 1) * LANES] = piece.astype(v_ref.dtype)
        _store_planes(planes_ref, _dot(xn, w_ref[:, SSM_WIDTH:]))

    @pl.when(jnp.logical_and(j == 1, s == 0))
    def _():
        tail_ref[...] = jnp.zeros_like(tail_ref)

    @pl.when(j == 1)
    def _conv():
        xn = xn_ref[...]
        row = lax.broadcasted_iota(jnp.int32, (ts, 1), 0)
        for c in range(CONV_WIDTH // cblk):
            cs = slice(c * cblk, (c + 1) * cblk)
            vs = slice(CONV_WIDTH + c * cblk, CONV_WIDTH + (c + 1) * cblk)
            z = _dot(xn, w_ref[:, cs]) * _dot(xn, w_ref[:, vs])
            prev1 = tail_ref[SUBLANES - 1:SUBLANES, cs]
            prev2 = tail_ref[SUBLANES - 2:SUBLANES - 1, cs]
            z1 = jnp.where(row == 0, prev1, pltpu.roll(z, 1, axis=0))
            z2 = jnp.where(row == 0, prev2, jnp.where(row == 1, prev1, pltpu.roll(z, 2, axis=0)))
            conv = cb_ref[:, cs] + cw_ref[0:1, cs] * z2 + cw_ref[1:2, cs] * z1 + cw_ref[2:3, cs] * z
            b_gate = _load_planes(planes_ref, c * cblk // LANES, (c + 1) * cblk // LANES)
            yb_ref[:, cs] = (b_gate * conv).astype(yb_ref.dtype)
            tail_ref[:, cs] = z[ts - SUBLANES:, :]

    @pl.when(j >= GATE_GROUP0)
    def _gates():
        gate_ref[...] = jax.nn.sigmoid(_dot(xn_ref[...], w_ref[...])).astype(gate_ref.dtype)


def _in_proj(h, g, w, conv_w, conv_b, *, batch, seq, ts=1024, cblk=256):
    t, d = h.shape
    nseq = seq // ts
    gw = 2 * CONV_WIDTH
    ngroups = w.shape[1] // gw
    tok = lambda b, s, j: (b * nseq + s, 0)
    const = lambda b, s, j: (0, 0)
    return pl.pallas_call(
        functools.partial(_in_proj_kernel, cblk=cblk),
        grid=(batch, nseq, ngroups),
        in_specs=[pl.BlockSpec((ts, d), tok),
                  pl.BlockSpec((1, d), const),
                  pl.BlockSpec((d, gw), lambda b, s, j: (0, j)),
                  pl.BlockSpec((CONV_K, CONV_WIDTH), const),
                  pl.BlockSpec((1, CONV_WIDTH), const)],
        out_specs=(pl.BlockSpec((PLANES, ts // CHUNK, CHUNK * LANES),
                                lambda b, s, j: (0, b * nseq + s, 0)),
                   pl.BlockSpec((ts, CONV_WIDTH), tok),
                   pl.BlockSpec((ts, gw),
                                lambda b, s, j: (b * nseq + s, jnp.maximum(j - GATE_GROUP0, 0)))),
        out_shape=(jax.ShapeDtypeStruct((PLANES, t // CHUNK, CHUNK * LANES), BF16),
                   jax.ShapeDtypeStruct((t, CONV_WIDTH), BF16),
                   jax.ShapeDtypeStruct((t, 2 * d), BF16)),
        scratch_shapes=[pltpu.VMEM((ts, d), BF16),
                        pltpu.VMEM((PLANES, ts, LANES), F32),
                        pltpu.VMEM((SUBLANES, CONV_WIDTH), F32)],
        compiler_params=pltpu.CompilerParams(
            dimension_semantics=("arbitrary", "arbitrary", "arbitrary"),
            vmem_limit_bytes=VMEM_LIMIT_IN_PROJ),
        name="in_proj",
    )(h, g, w, conv_w, conv_b)


def _block_diag_tile(src, period, rows_per_group, cols_per_group):
    nrows = src.shape[0]
    ncols = cols_per_group * GROUPS_PER_PLANE
    k = lax.broadcasted_iota(jnp.int32, (src.shape[1], ncols), 0)
    c = lax.broadcasted_iota(jnp.int32, (src.shape[1], ncols), 1)
    tiled = _dot_f32(src, (k == c % period).astype(F32))
    r = lax.broadcasted_iota(jnp.int32, (nrows, ncols), 0)
    c = lax.broadcasted_iota(jnp.int32, (nrows, ncols), 1)
    return jnp.where(r // rows_per_group == c // cols_per_group, tiled, 0.0)


def _ssm_kernel(x_ref, btr_ref, bti_ref, ctr_ref, cti_ref, arow_ref, acol_ref, d_ref, o_ref,
                toep_ref, win_ref, wout_ref, y_ref, *, chunks):
    half = HALF_STATE

    @pl.when(pl.program_id(1) == 0)
    def _build_operators():
        btr = _block_diag_tile(btr_ref[...], SSM_STATE, SSM_GROUP, SSM_STATE)
        bti = _block_diag_tile(bti_ref[...], SSM_STATE, SSM_GROUP, SSM_STATE)
        ctr = _block_diag_tile(ctr_ref[...], SSM_GROUP, SSM_STATE, SSM_GROUP)
        cti = _block_diag_tile(cti_ref[...], SSM_GROUP, SSM_STATE, SSM_GROUP)
        ctr_parts = _split_bf16(ctr)
        cti_parts = _split_bf16(cti)
        zero = jnp.zeros((LANES, LANES), BF16)
        for k in range(CHUNK):
            ar = arow_ref[k:k + 1, :half]
            ai = arow_ref[k:k + 1, half:]
            wr = btr * ar - bti * ai
            wi = btr * ai + bti * ar
            rows = slice((CHUNK - 1 - k) * LANES, (CHUNK - k) * LANES)
            win_ref[rows, :half] = wr.astype(BF16)
            win_ref[rows, half:] = wi.astype(BF16)
            lag = (_dot_split(wr, ctr_parts) - _dot_split(wi, cti_parts)).astype(BF16)
            for lin in range(CHUNK - k):
                lout = lin + k
                toep_ref[lin * LANES:(lin + 1) * LANES, lout * LANES:(lout + 1) * LANES] = lag
        for lin in range(1, CHUNK, MXU_DIM // LANES):
            toep_ref[lin * LANES:(lin + 1) * LANES, (lin - 1) * LANES:lin * LANES] = zero
        for l in range(CHUNK):
            arc = acol_ref[:, l + 1:l + 2]
            aic = acol_ref[:, ACOL_IM + l + 1:ACOL_IM + l + 2]
            cols = slice(l * LANES, (l + 1) * LANES)
            wout_ref[:half, cols] = (ctr * arc - cti * aic).astype(BF16)
            wout_ref[half:, cols] = (-(ctr * aic + cti * arc)).astype(BF16)

    rows, width = x_ref.shape
    s_in = _dot(x_ref[...], win_ref[...])
    sr = s_in[:, :half]
    si = s_in[:, half:]
    for cb in range(width // MXU_DIM):
        cols = slice(cb * MXU_DIM, (cb + 1) * MXU_DIM)
        kk = (cb + 1) * MXU_DIM
        y_ref[:, cols] = (_dot(x_ref[:, :kk], toep_ref[:kk, cols])
                          + d_ref[:, cols] * x_ref[:, cols].astype(F32))
    cidx = lax.broadcasted_iota(jnp.int32, (rows, 1), 0) % chunks
    for k in range(SCAN_STEPS_IN_GROUP):
        sh = 1 << k
        ar = arow_ref[SCAN_ROW0 + k:SCAN_ROW0 + k + 1, :half]
        ai = arow_ref[SCAN_ROW0 + k:SCAN_ROW0 + k + 1, half:]
        keep = cidx % SUBLANES >= sh
        pr = jnp.where(keep, pltpu.roll(sr, sh, axis=0), 0.0)
        pi = jnp.where(keep, pltpu.roll(si, sh, axis=0), 0.0)
        sr, si = sr + (ar * pr - ai * pi), si + (ar * pi + ai * pr)
    cr = arow_ref[CARRY_ROW0:CARRY_ROW0 + SUBLANES, :half]
    ci = arow_ref[CARRY_ROW0:CARRY_ROW0 + SUBLANES, half:]
    out_r, out_i = [], []
    for g in range(rows // SUBLANES):
        gr = sr[g * SUBLANES:(g + 1) * SUBLANES, :]
        gi = si[g * SUBLANES:(g + 1) * SUBLANES, :]
        if g % (chunks // SUBLANES):
            lr = jnp.broadcast_to(out_r[-1][SUBLANES - 1:, :], gr.shape)
            li = jnp.broadcast_to(out_i[-1][SUBLANES - 1:, :], gi.shape)
            gr, gi = gr + (cr * lr - ci * li), gi + (cr * li + ci * lr)
        out_r.append(gr)
        out_i.append(gi)
    sr = jnp.concatenate(out_r, axis=0)
    si = jnp.concatenate(out_i, axis=0)
    first = cidx >= 1
    pr = jnp.where(first, pltpu.roll(sr, 1, axis=0), 0.0).astype(BF16)
    pi = jnp.where(first, pltpu.roll(si, 1, axis=0), 0.0).astype(BF16)
    s_prev = jnp.concatenate([pr, pi], axis=1)
    for cb in range(width // MXU_DIM):
        cols = slice(cb * MXU_DIM, (cb + 1) * MXU_DIM)
        y = y_ref[:, cols] + _dot(s_prev, wout_ref[:, cols])
        o_ref[:, cols] = jax.nn.gelu(y).astype(o_ref.dtype)


def _ssm(xc, btr, bti, ctr, cti, arow, acol, d_tiled, *, chunks, rows=1024):
    planes, nrows, width = xc.shape
    nh = nrows // rows

    def plane(*shape):
        return pl.BlockSpec((None,) + shape, lambda j, h: (j,) + (0,) * len(shape))

    return pl.pallas_call(
        functools.partial(_ssm_kernel, chunks=chunks),
        grid=(planes, nh),
        in_specs=[
            pl.BlockSpec((None, rows, width), lambda j, h: (j, h, 0)),
            plane(LANES, LANES), plane(LANES, LANES),
            plane(HALF_STATE, LANES), plane(HALF_STATE, LANES),
            plane(APOW_ROWS, PLANE_STATE), plane(HALF_STATE, LANES),
            plane(1, width),
        ],
        out_specs=pl.BlockSpec((None, rows, width), lambda j, h: (j, h, 0)),
        out_shape=jax.ShapeDtypeStruct((planes, nrows, width), BF16),
        scratch_shapes=[pltpu.VMEM((width, width), BF16),
                        pltpu.VMEM((width, PLANE_STATE), BF16),
                        pltpu.VMEM((PLANE_STATE, width), BF16),
                        pltpu.VMEM((rows, width), F32)],
        compiler_params=pltpu.CompilerParams(
            dimension_semantics=("arbitrary", "arbitrary"), vmem_limit_bytes=VMEM_LIMIT),
        name="ssm_scan",
    )(xc, btr, bti, ctr, cti, arow, acol, d_tiled)


def _ssm_tables(lam_re, lam_im, log_dt, b_re, b_im, c_re, c_im, d_skip):
    lam_re = jnp.minimum(lam_re, -1e-4)
    dt = jnp.exp(log_dt)[:, None]
    mag = jnp.exp(lam_re * dt)
    a_re = mag * jnp.cos(lam_im * dt)
    a_im = mag * jnp.sin(lam_im * dt)
    den = lam_re * lam_re + lam_im * lam_im
    p = a_re - 1.0
    f_re = ((p * lam_re + a_im * lam_im) / den)[:, :, None]
    f_im = ((a_im * lam_re - p * lam_im) / den)[:, :, None]
    bb_re = f_re * b_re - f_im * b_im
    bb_im = f_re * b_im + f_im * b_re

    N, C, P, L = SSM_STATE, SSM_GROUP, PLANES, CHUNK

    def bt(bb):
        m = jnp.transpose(bb, (0, 2, 1)).reshape(P, LANES, N)
        return jnp.pad(m, ((0, 0), (0, 0), (0, LANES - N)))

    def ct(c):
        m = jnp.transpose(c, (0, 2, 1)).reshape(P, HALF_STATE, C)
        return jnp.pad(m, ((0, 0), (0, 0), (0, LANES - C)))

    ks = (list(range(L + 1)) + [L << m for m in range(SCAN_STEPS_IN_GROUP)])
    ks += [0] * (CARRY_ROW0 - len(ks)) + [L * (r + 1) for r in range(SUBLANES)]
    ks = jnp.asarray(ks, F32).reshape(-1, 1, 1)
    m = jnp.exp(ks * (lam_re * dt))
    pr = m * jnp.cos(ks * (lam_im * dt))
    pi = m * jnp.sin(ks * (lam_im * dt))
    arow = jnp.concatenate([pr.reshape(APOW_ROWS, P, HALF_STATE),
                            pi.reshape(APOW_ROWS, P, HALF_STATE)], axis=2)
    arow = jnp.transpose(arow, (1, 0, 2))

    def col(pw):
        c = jnp.transpose(pw[:L + 1].reshape(L + 1, P, HALF_STATE), (1, 2, 0))
        return jnp.pad(c, ((0, 0), (0, 0), (0, ACOL_IM - (L + 1))))

    acol = jnp.concatenate([col(pr), col(pi)], axis=2)
    d_tiled = jnp.tile(d_skip.reshape(P, 1, LANES), (1, 1, L))
    return bt(bb_re), bt(bb_im), ct(c_re), ct(c_im), arow, acol, d_tiled


def _mix_out_kernel(g_ref, yb_ref, gate_ref, h_ref, wglu_ref, bglu_ref, wa_ref, wb_ref, wo_ref, o_ref,
                    gs_ref):
    nchunk = g_ref.shape[1]
    for p in range(PLANES):
        for l in range(CHUNK):
            piece = g_ref[p, :, l * LANES:(l + 1) * LANES].astype(F32)
            gs_ref[p, pl.ds(l, nchunk, stride=CHUNK), :] = piece
    g = _load_planes(gs_ref, 0, PLANES)
    glu = _dot(g.astype(BF16), wglu_ref[...]) + bglu_ref[...]
    y_a = (g * jax.nn.sigmoid(glu)).astype(BF16)
    z_a = _dot(y_a, wa_ref[...])
    z_b = _dot(yb_ref[...], wb_ref[...])
    merged = (gate_ref[:, :D_MODEL].astype(F32) * z_a
              + gate_ref[:, D_MODEL:].astype(F32) * z_b).astype(BF16)
    o_ref[...] = h_ref[...] + _dot(merged, wo_ref[...])


def _mix_out(g_planes, y_b, gates, h, w_glu, b_glu, w_a, w_b, w_o, *, tm=512):
    t, d = h.shape

    def const(shape):
        return pl.BlockSpec(shape, lambda i: (0,) * len(shape), pipeline_mode=pl.Buffered(1))

    return pl.pallas_call(
        _mix_out_kernel,
        grid=(t // tm,),
        in_specs=[
            pl.BlockSpec((PLANES, tm // CHUNK, CHUNK * LANES), lambda i: (0, i, 0)),
            pl.BlockSpec((tm, CONV_WIDTH), lambda i: (i, 0)),
            pl.BlockSpec((tm, 2 * d), lambda i: (i, 0)),
            pl.BlockSpec((tm, d), lambda i: (i, 0)),
            const((SSM_WIDTH, SSM_WIDTH)), const((1, SSM_WIDTH)),
            const((SSM_WIDTH, d)), const((CONV_WIDTH, d)), const((d, d)),
        ],
        out_specs=pl.BlockSpec((tm, d), lambda i: (i, 0)),
        out_shape=jax.ShapeDtypeStruct((t, d), F32),
        scratch_shapes=[pltpu.VMEM((PLANES, tm, LANES), F32)],
        compiler_params=pltpu.CompilerParams(
            dimension_semantics=("parallel",), vmem_limit_bytes=VMEM_LIMIT),
        name="mix_out",
    )(g_planes, y_b, gates, h, w_glu, b_glu, w_a, w_b, w_o)


def kernel(x, ffn1_norm, ffn1_w_gate, ffn1_w_up, ffn1_w_down, mix_norm, w_in, ssm_lambda_re, ssm_lambda_im, ssm_log_dt, ssm_b_re, ssm_b_im, ssm_c_re, ssm_c_im, ssm_d, ssm_w_glu, ssm_b_glu, ssm_w_out, conv_w, conv_b, conv_w_out, w_o, ffn2_norm, ffn2_w_gate, ffn2_w_up, ffn2_w_down, final_norm):
    batch, seq, d = x.shape
    t = batch * seq
    chunks = seq // CHUNK

    bf = lambda w: w.astype(BF16)
    vec = lambda g: g.reshape(1, -1).astype(F32)

    later = (ffn2_w_gate, ffn2_w_up, ffn2_w_down, w_in, ssm_w_glu, ssm_w_out, conv_w_out, w_o)
    h1, wg2, wu2, wd2, w_in_b, w_glu_b, w_a_b, w_b_b, w_o_b = _ffn(
        x.reshape(t, d), vec(ffn1_norm), bf(ffn1_w_gate), bf(ffn1_w_up), bf(ffn1_w_down),
        cast=later)

    v_chunks, y_b, gates = _in_proj(h1, vec(mix_norm), w_in_b, conv_w.astype(F32), vec(conv_b),
                                    batch=batch, seq=seq)

    tables = _ssm_tables(ssm_lambda_re, ssm_lambda_im, ssm_log_dt, ssm_b_re, ssm_b_im,
                         ssm_c_re, ssm_c_im, ssm_d)
    g_planes = _ssm(v_chunks, *tables, chunks=chunks)

    h2 = _mix_out(g_planes, y_b, gates, h1, w_glu_b, vec(ssm_b_glu), w_a_b, w_b_b, w_o_b)

    out = _ffn(h2, vec(ffn2_norm), wg2, wu2, wd2, vec(final_norm))
    return out.reshape(batch, seq, d)
```

```python
import functools

import jax
import jax.numpy as jnp
from jax import lax
from jax.experimental import pallas as pl
from jax.experimental.pallas import tpu as pltpu

D_MODEL = 2048
SSM_WIDTH = 1024
SSM_GROUP = 16
SSM_GROUPS = 64
SSM_STATE = 64
CONV_WIDTH = 1024
CONV_K = 3
EPS = 1e-6

LANES = 128
SUBLANES = 8
BF16_SUBLANES = 16
MXU_DIM = 256
CHUNK = 16
PLANES = SSM_WIDTH // LANES
GROUPS_PER_PLANE = LANES // SSM_GROUP
HALF_STATE = GROUPS_PER_PLANE * SSM_STATE
PLANE_STATE = 2 * HALF_STATE
SCAN_STEPS_IN_GROUP = 3
SCAN_ROW0 = CHUNK + 1
CARRY_ROW0 = -(-(SCAN_ROW0 + SCAN_STEPS_IN_GROUP) // SUBLANES) * SUBLANES
APOW_ROWS = CARRY_ROW0 + SUBLANES
ACOL_IM = LANES // 2
VMEM_BYTES = 64 * 1024 * 1024
VMEM_LIMIT = VMEM_BYTES - 8 * 1024 * 1024
VMEM_LIMIT_FFN = VMEM_BYTES - 4 * 1024 * 1024
VMEM_LIMIT_IN_PROJ = VMEM_BYTES - 2 * 1024 * 1024

BF16 = jnp.bfloat16
F32 = jnp.float32
HIGHEST = lax.Precision.HIGHEST


def _rms(x, g):
    return x * lax.rsqrt(jnp.mean(x * x, axis=-1, keepdims=True) + EPS) * g


def _dot(a, b):
    return jnp.dot(a, b, preferred_element_type=F32)


def _dot_f32(a, b):
    return jnp.dot(a, b, precision=HIGHEST, preferred_element_type=F32)


def _split_bf16(a):
    hi = a.astype(BF16)
    return hi, (a - hi.astype(F32)).astype(BF16)


def _dot_split(a, b_parts):
    a_hi, a_lo = _split_bf16(a)
    b_hi, b_lo = b_parts
    return _dot(a_hi, b_hi) + (_dot(a_lo, b_hi) + _dot(a_hi, b_lo))


ROW_SPLIT = 4


def _staggered_row_specs(tile_rows, ncols, ntiles, nsteps, tile_and_step):
    specs = []
    for q in range(ROW_SPLIT):
        def index_map(*grid, q=q):
            tile, step = tile_and_step(*grid)
            ahead = (step >= nsteps - (ROW_SPLIT - 1 - q)).astype(jnp.int32)
            return (jnp.minimum(tile + ahead, ntiles - 1) * ROW_SPLIT + q, 0)
        specs.append(pl.BlockSpec((tile_rows // ROW_SPLIT, ncols), index_map))
    return specs


def _ffn_kernel(*refs, final_norm, d_ff, n_cast):
    x_parts = refs[:ROW_SPLIT]
    gin_ref, wg_ref, wu_ref, wd_ref, *refs = refs[ROW_SPLIT:]
    if final_norm:
        gout_ref, *refs = refs
    cast_src = refs[:n_cast]
    o_ref = refs[n_cast]
    cast_dst = refs[n_cast + 1:2 * n_cast + 1]
    xn_ref = refs[2 * n_cast + 1]
    f = pl.program_id(1)
    nf = pl.num_programs(1)
    tf = wd_ref.shape[0]

    @pl.when(f == 0)
    def _():
        part_rows = x_parts[0].shape[0]
        for q, x_ref in enumerate(x_parts):
            rows = slice(q * part_rows, (q + 1) * part_rows)
            x = x_ref[...]
            xn_ref[rows, :] = _rms(x, gin_ref[...]).astype(BF16)
            o_ref[rows, :] = x

    def accumulate(width):
        xn = xn_ref[...]
        gate = _dot(xn, wg_ref[:, :width])
        up = _dot(xn, wu_ref[:, :width])
        act = ((0.5 * gate) * jax.nn.sigmoid(gate) * up).astype(BF16)
        o_ref[...] += _dot(act, wd_ref[:width, :])

    last_width = d_ff - (pl.cdiv(d_ff, tf) - 1) * tf
    if last_width == tf:
        accumulate(tf)
    else:
        pl.when(f < nf - 1)(functools.partial(accumulate, tf))
        pl.when(f == nf - 1)(functools.partial(accumulate, last_width))

    if final_norm:
        @pl.when(f == nf - 1)
        def _():
            o_ref[...] = _rms(o_ref[...], gout_ref[...])

    for src, dst in zip(cast_src, cast_dst):
        dst[...] = src[...].astype(dst.dtype)


def _cast_block_rows(nrows, steps):
    br = BF16_SUBLANES
    while nrows % br or nrows // br > steps:
        br += BF16_SUBLANES
    return br


def _ffn(x, g_in, wg, wu, wd, g_out=None, *, cast=(), tm=1024, tf=512):
    t, d = x.shape
    d_ff = wg.shape[1]
    final_norm = g_out is not None
    nt = t // tm
    nf = pl.cdiv(d_ff, tf)
    steps = nt * nf
    row = pl.BlockSpec((tm, d), lambda i, f: (i, 0))
    vec = pl.BlockSpec((1, d), lambda i, f: (0, 0))
    in_specs = _staggered_row_specs(tm, d, nt, nf, lambda i, f: (i, f)) + [
        vec,
        pl.BlockSpec((d, tf), lambda i, f: (0, f)),
        pl.BlockSpec((d, tf), lambda i, f: (0, f)),
        pl.BlockSpec((tf, d), lambda i, f: (f, 0))]
    args = [x] * ROW_SPLIT + [g_in, wg, wu, wd]
    if final_norm:
        in_specs.append(vec)
        args.append(g_out)
    def cast_specs():
        specs = []
        for w in cast:
            nrows, ncols = w.shape
            br = _cast_block_rows(nrows, steps)
            specs.append(pl.BlockSpec(
                (br, ncols), lambda i, f, nb=nrows // br: (jnp.minimum(i * nf + f, nb - 1), 0)))
        return specs

    outs = pl.pallas_call(
        functools.partial(_ffn_kernel, final_norm=final_norm, d_ff=d_ff, n_cast=len(cast)),
        grid=(nt, nf),
        in_specs=in_specs + cast_specs(),
        out_specs=[row] + cast_specs(),
        out_shape=[jax.ShapeDtypeStruct((t, d), F32)]
        + [jax.ShapeDtypeStruct(w.shape, BF16) for w in cast],
        scratch_shapes=[pltpu.VMEM((tm, d), BF16)],
        compiler_params=pltpu.CompilerParams(
            dimension_semantics=("arbitrary", "arbitrary"),
            vmem_limit_bytes=VMEM_LIMIT_FFN),
        name="ffn_final" if final_norm else "ffn_hidden",
    )(*args, *cast)
    return outs if cast else outs[0]


GATE_GROUP0 = 2


def _store_planes(ref, val):
    for p in range(ref.shape[0]):
        ref[p] = val[:, p * LANES:(p + 1) * LANES]


def _load_planes(ref, p0, p1):
    return jnp.concatenate([ref[p] for p in range(p0, p1)], axis=1)


def _in_proj_kernel(*refs, cblk):
    h_parts = refs[:ROW_SPLIT]
    (g_ref, w_ref, cw_ref, cb_ref, v_ref, yb_ref, gate_ref,
     xn_ref, planes_ref, tail_ref) = refs[ROW_SPLIT:]
    s = pl.program_id(1)
    j = pl.program_id(2)
    ts = xn_ref.shape[0]
    nchunk = ts // CHUNK

    @pl.when(j == 0)
    def _ssm_input_and_b_gate():
        part_rows = ts // ROW_SPLIT
        for q, h_ref in enumerate(h_parts):
            xn_ref[q * part_rows:(q + 1) * part_rows, :] = _rms(h_ref[...], g_ref[...]).astype(BF16)
        xn = xn_ref[...]
        _store_planes(planes_ref, _dot(xn, w_ref[:, :SSM_WIDTH]))
        for p in range(PLANES):
            for l in range(CHUNK):
                piece = planes_ref[p, pl.ds(l, nchunk, stride=CHUNK), :]
                v_ref[p, :, l * LANES:(l + 1) * LANES] = piece.astype(v_ref.dtype)
        _store_planes(planes_ref, _dot(xn, w_ref[:, SSM_WIDTH:]))

    @pl.when(jnp.logical_and(j == 1, s == 0))
    def _():
        tail_ref[...] = jnp.zeros_like(tail_ref)

    @pl.when(j == 1)
    def _conv():
        xn = xn_ref[...]
        row = lax.broadcasted_iota(jnp.int32, (ts, 1), 0)
        for c in range(CONV_WIDTH // cblk):
            cs = slice(c * cblk, (c + 1) * cblk)
            vs = slice(CONV_WIDTH + c * cblk, CONV_WIDTH + (c + 1) * cblk)
            z = _dot(xn, w_ref[:, cs]) * _dot(xn, w_ref[:, vs])
            prev1 = tail_ref[SUBLANES - 1:SUBLANES, cs]
            prev2 = tail_ref[SUBLANES - 2:SUBLANES - 1, cs]
            z1 = jnp.where(row == 0, prev1, pltpu.roll(z, 1, axis=0))
            z2 = jnp.where(row == 0, prev2, jnp.where(row == 1, prev1, pltpu.roll(z, 2, axis=0)))
            conv = cb_ref[:, cs] + cw_ref[0:1, cs] * z2 + cw_ref[1:2, cs] * z1 + cw_ref[2:3, cs] * z
            b_gate = _load_planes(planes_ref, c * cblk // LANES, (c + 1) * cblk // LANES)
            yb_ref[:, cs] = (b_gate * conv).astype(yb_ref.dtype)
            tail_ref[:, cs] = z[ts - SUBLANES:, :]

    @pl.when(j >= GATE_GROUP0)
    def _gates():
        gate_ref[...] = jax.nn.sigmoid(_dot(xn_ref[...], w_ref[...])).astype(gate_ref.dtype)


def _in_proj(h, g, w, conv_w, conv_b, *, batch, seq, ts=1024, cblk=256):
    t, d = h.shape
    nseq = seq // ts
    gw = 2 * CONV_WIDTH
    ngroups = w.shape[1] // gw
    tok = lambda b, s, j: (b * nseq + s, 0)
    const = lambda b, s, j: (0, 0)
    return pl.pallas_call(
        functools.partial(_in_proj_kernel, cblk=cblk),
        grid=(batch, nseq, ngroups),
        in_specs=_staggered_row_specs(ts, d, batch * nseq, ngroups,
                                      lambda b, s, j: (b * nseq + s, j)) + [
                  pl.BlockSpec((1, d), const),
                  pl.BlockSpec((d, gw), lambda b, s, j: (0, j)),
                  pl.BlockSpec((CONV_K, CONV_WIDTH), const),
                  pl.BlockSpec((1, CONV_WIDTH), const)],
        out_specs=(pl.BlockSpec((PLANES, ts // CHUNK, CHUNK * LANES),
                                lambda b, s, j: (0, b * nseq + s, 0)),
                   pl.BlockSpec((ts, CONV_WIDTH), tok),
                   pl.BlockSpec((ts, gw),
                                lambda b, s, j: (b * nseq + s, jnp.maximum(j - GATE_GROUP0, 0)))),
        out_shape=(jax.ShapeDtypeStruct((PLANES, t // CHUNK, CHUNK * LANES), BF16),
                   jax.ShapeDtypeStruct((t, CONV_WIDTH), BF16),
                   jax.ShapeDtypeStruct((t, 2 * d), BF16)),
        scratch_shapes=[pltpu.VMEM((ts, d), BF16),
                        pltpu.VMEM((PLANES, ts, LANES), F32),
                        pltpu.VMEM((SUBLANES, CONV_WIDTH), F32)],
        compiler_params=pltpu.CompilerParams(
            dimension_semantics=("arbitrary", "arbitrary", "arbitrary"),
            vmem_limit_bytes=VMEM_LIMIT_IN_PROJ),
        name="in_proj",
    )(*[h] * ROW_SPLIT, g, w, conv_w, conv_b)


def _block_diag_tile(src, period, rows_per_group, cols_per_group):
    nrows = src.shape[0]
    ncols = cols_per_group * GROUPS_PER_PLANE
    k = lax.broadcasted_iota(jnp.int32, (src.shape[1], ncols), 0)
    c = lax.broadcasted_iota(jnp.int32, (src.shape[1], ncols), 1)
    tiled = _dot_f32(src, (k == c % period).astype(F32))
    r = lax.broadcasted_iota(jnp.int32, (nrows, ncols), 0)
    c = lax.broadcasted_iota(jnp.int32, (nrows, ncols), 1)
    return jnp.where(r // rows_per_group == c // cols_per_group, tiled, 0.0)


def _ssm_kernel(x_ref, btr_ref, bti_ref, ctr_ref, cti_ref, arow_ref, acol_ref, d_ref, o_ref,
                toep_ref, win_ref, wout_ref, y_ref, *, chunks):
    half = HALF_STATE

    @pl.when(pl.program_id(1) == 0)
    def _build_operators():
        btr = _block_diag_tile(btr_ref[...], SSM_STATE, SSM_GROUP, SSM_STATE)
        bti = _block_diag_tile(bti_ref[...], SSM_STATE, SSM_GROUP, SSM_STATE)
        ctr = _block_diag_tile(ctr_ref[...], SSM_GROUP, SSM_STATE, SSM_GROUP)
        cti = _block_diag_tile(cti_ref[...], SSM_GROUP, SSM_STATE, SSM_GROUP)
        ctr_parts = _split_bf16(ctr)
        cti_parts = _split_bf16(cti)
        zero = jnp.zeros((LANES, LANES), BF16)
        for k in range(CHUNK):
            ar = arow_ref[k:k + 1, :half]
            ai = arow_ref[k:k + 1, half:]
            wr = btr * ar - bti * ai
            wi = btr * ai + bti * ar
            rows = slice((CHUNK - 1 - k) * LANES, (CHUNK - k) * LANES)
            win_ref[rows, :half] = wr.astype(BF16)
            win_ref[rows, half:] = wi.astype(BF16)
            lag = (_dot_split(wr, ctr_parts) - _dot_split(wi, cti_parts)).astype(BF16)
            for lin in range(CHUNK - k):
                lout = lin + k
                toep_ref[lin * LANES:(lin + 1) * LANES, lout * LANES:(lout + 1) * LANES] = lag
        for lin in range(1, CHUNK, MXU_DIM // LANES):
            toep_ref[lin * LANES:(lin + 1) * LANES, (lin - 1) * LANES:lin * LANES] = zero
        for l in range(CHUNK):
            arc = acol_ref[:, l + 1:l + 2]
            aic = acol_ref[:, ACOL_IM + l + 1:ACOL_IM + l + 2]
            cols = slice(l * LANES, (l + 1) * LANES)
            wout_ref[:half, cols] = (ctr * arc - cti * aic).astype(BF16)
            wout_ref[half:, cols] = (-(ctr * aic + cti * arc)).astype(BF16)

    rows, width = x_ref.shape
    s_in = _dot(x_ref[...], win_ref[...])
    sr = s_in[:, :half]
    si = s_in[:, half:]
    for cb in range(width // MXU_DIM):
        cols = slice(cb * MXU_DIM, (cb + 1) * MXU_DIM)
        kk = (cb + 1) * MXU_DIM
        y_ref[:, cols] = (_dot(x_ref[:, :kk], toep_ref[:kk, cols])
                          + d_ref[:, cols] * x_ref[:, cols].astype(F32))
    cidx = lax.broadcasted_iota(jnp.int32, (rows, 1), 0) % chunks
    for k in range(SCAN_STEPS_IN_GROUP):
        sh = 1 << k
        ar = arow_ref[SCAN_ROW0 + k:SCAN_ROW0 + k + 1, :half]
        ai = arow_ref[SCAN_ROW0 + k:SCAN_ROW0 + k + 1, half:]
        keep = cidx % SUBLANES >= sh
        pr = jnp.where(keep, pltpu.roll(sr, sh, axis=0), 0.0)
        pi = jnp.where(keep, pltpu.roll(si, sh, axis=0), 0.0)
        sr, si = sr + (ar * pr - ai * pi), si + (ar * pi + ai * pr)
    cr = arow_ref[CARRY_ROW0:CARRY_ROW0 + SUBLANES, :half]
    ci = arow_ref[CARRY_ROW0:CARRY_ROW0 + SUBLANES, half:]
    out_r, out_i = [], []
    for g in range(rows // SUBLANES):
        gr = sr[g * SUBLANES:(g + 1) * SUBLANES, :]
        gi = si[g * SUBLANES:(g + 1) * SUBLANES, :]
        if g % (chunks // SUBLANES):
            lr = jnp.broadcast_to(out_r[-1][SUBLANES - 1:, :], gr.shape)
            li = jnp.broadcast_to(out_i[-1][SUBLANES - 1:, :], gi.shape)
            gr, gi = gr + (cr * lr - ci * li), gi + (cr * li + ci * lr)
        out_r.append(gr)
        out_i.append(gi)
    sr = jnp.concatenate(out_r, axis=0)
    si = jnp.concatenate(out_i, axis=0)
    first = cidx >= 1
    pr = jnp.where(first, pltpu.roll(sr, 1, axis=0), 0.0).astype(BF16)
    pi = jnp.where(first, pltpu.roll(si, 1, axis=0), 0.0).astype(BF16)
    s_prev = jnp.concatenate([pr, pi], axis=1)
    for cb in range(width // MXU_DIM):
        cols = slice(cb * MXU_DIM, (cb + 1) * MXU_DIM)
        y = y_ref[:, cols] + _dot(s_prev, wout_ref[:, cols])
        o_ref[:, cols] = jax.nn.gelu(y).astype(o_ref.dtype)


def _ssm(xc, btr, bti, ctr, cti, arow, acol, d_tiled, *, chunks, rows=1024):
    planes, nrows, width = xc.shape
    nh = nrows // rows

    def plane(*shape):
        return pl.BlockSpec((None,) + shape, lambda j, h: (j,) + (0,) * len(shape))

    return pl.pallas_call(
        functools.partial(_ssm_kernel, chunks=chunks),
        grid=(planes, nh),
        in_specs=[
            pl.BlockSpec((None, rows, width), lambda j, h: (j, h, 0)),
            plane(LANES, LANES), plane(LANES, LANES),
            plane(HALF_STATE, LANES), plane(HALF_STATE, LANES),
            plane(APOW_ROWS, PLANE_STATE), plane(HALF_STATE, LANES),
            plane(1, width),
        ],
        out_specs=pl.BlockSpec((None, rows, width), lambda j, h: (j, h, 0)),
        out_shape=jax.ShapeDtypeStruct((planes, nrows, width), BF16),
        scratch_shapes=[pltpu.VMEM((width, width), BF16),
                        pltpu.VMEM((width, PLANE_STATE), BF16),
                        pltpu.VMEM((PLANE_STATE, width), BF16),
                        pltpu.VMEM((rows, width), F32)],
        compiler_params=pltpu.CompilerParams(
            dimension_semantics=("arbitrary", "arbitrary"), vmem_limit_bytes=VMEM_LIMIT),
        name="ssm_scan",
    )(xc, btr, bti, ctr, cti, arow, acol, d_tiled)


def _ssm_tables(lam_re, lam_im, log_dt, b_re, b_im, c_re, c_im, d_skip):
    lam_re = jnp.minimum(lam_re, -1e-4)
    dt = jnp.exp(log_dt)[:, None]
    mag = jnp.exp(lam_re * dt)
    a_re = mag * jnp.cos(lam_im * dt)
    a_im = mag * jnp.sin(lam_im * dt)
    den = lam_re * lam_re + lam_im * lam_im
    p = a_re - 1.0
    f_re = ((p * lam_re + a_im * lam_im) / den)[:, :, None]
    f_im = ((a_im * lam_re - p * lam_im) / den)[:, :, None]
    bb_re = f_re * b_re - f_im * b_im
    bb_im = f_re * b_im + f_im * b_re

    N, C, P, L = SSM_STATE, SSM_GROUP, PLANES, CHUNK

    def bt(bb):
        m = jnp.transpose(bb, (0, 2, 1)).reshape(P, LANES, N)
        return jnp.pad(m, ((0, 0), (0, 0), (0, LANES - N)))

    def ct(c):
        m = jnp.transpose(c, (0, 2, 1)).reshape(P, HALF_STATE, C)
        return jnp.pad(m, ((0, 0), (0, 0), (0, LANES - C)))

    ks = (list(range(L + 1)) + [L << m for m in range(SCAN_STEPS_IN_GROUP)])
    ks += [0] * (CARRY_ROW0 - len(ks)) + [L * (r + 1) for r in range(SUBLANES)]
    ks = jnp.asarray(ks, F32).reshape(-1, 1, 1)
    m = jnp.exp(ks * (lam_re * dt))
    pr = m * jnp.cos(ks * (lam_im * dt))
    pi = m * jnp.sin(ks * (lam_im * dt))
    arow = jnp.concatenate([pr.reshape(APOW_ROWS, P, HALF_STATE),
                            pi.reshape(APOW_ROWS, P, HALF_STATE)], axis=2)
    arow = jnp.transpose(arow, (1, 0, 2))

    def col(pw):
        c = jnp.transpose(pw[:L + 1].reshape(L + 1, P, HALF_STATE), (1, 2, 0))
        return jnp.pad(c, ((0, 0), (0, 0), (0, ACOL_IM - (L + 1))))

    acol = jnp.concatenate([col(pr), col(pi)], axis=2)
    d_tiled = jnp.tile(d_skip.reshape(P, 1, LANES), (1, 1, L))
    return bt(bb_re), bt(bb_im), ct(c_re), ct(c_im), arow, acol, d_tiled


def _mix_out_kernel(g_ref, yb_ref, gate_ref, h_ref, wglu_ref, bglu_ref, wa_ref, wb_ref, wo_ref, o_ref,
                    gs_ref):
    nchunk = g_ref.shape[1]
    for p in range(PLANES):
        for l in range(CHUNK):
            piece = g_ref[p, :, l * LANES:(l + 1) * LANES].astype(F32)
            gs_ref[p, pl.ds(l, nchunk, stride=CHUNK), :] = piece
    g = _load_planes(gs_ref, 0, PLANES)
    glu = _dot(g.astype(BF16), wglu_ref[...]) + bglu_ref[...]
    y_a = (g * jax.nn.sigmoid(glu)).astype(BF16)
    z_a = _dot(y_a, wa_ref[...])
    z_b = _dot(yb_ref[...], wb_ref[...])
    merged = (gate_ref[:, :D_MODEL].astype(F32) * z_a
              + gate_ref[:, D_MODEL:].astype(F32) * z_b).astype(BF16)
    o_ref[...] = h_ref[...] + _dot(merged, wo_ref[...])


def _mix_out(g_planes, y_b, gates, h, w_glu, b_glu, w_a, w_b, w_o, *, tm=512):
    t, d = h.shape

    def const(shape):
        return pl.BlockSpec(shape, lambda i: (0,) * len(shape), pipeline_mode=pl.Buffered(1))

    return pl.pallas_call(
        _mix_out_kernel,
        grid=(t // tm,),
        in_specs=[
            pl.BlockSpec((PLANES, tm // CHUNK, CHUNK * LANES), lambda i: (0, i, 0)),
            pl.BlockSpec((tm, CONV_WIDTH), lambda i: (i, 0)),
            pl.BlockSpec((tm, 2 * d), lambda i: (i, 0)),
            pl.BlockSpec((tm, d), lambda i: (i, 0)),
            const((SSM_WIDTH, SSM_WIDTH)), const((1, SSM_WIDTH)),
            const((SSM_WIDTH, d)), const((CONV_WIDTH, d)), const((d, d)),
        ],
        out_specs=pl.BlockSpec((tm, d), lambda i: (i, 0)),
        out_shape=jax.ShapeDtypeStruct((t, d), F32),
        scratch_shapes=[pltpu.VMEM((PLANES, tm, LANES), F32)],
        compiler_params=pltpu.CompilerParams(
            dimension_semantics=("parallel",), vmem_limit_bytes=VMEM_LIMIT),
        name="mix_out",
    )(g_planes, y_b, gates, h, w_glu, b_glu, w_a, w_b, w_o)


def kernel(x, ffn1_norm, ffn1_w_gate, ffn1_w_up, ffn1_w_down, mix_norm, w_in, ssm_lambda_re, ssm_lambda_im, ssm_log_dt, ssm_b_re, ssm_b_im, ssm_c_re, ssm_c_im, ssm_d, ssm_w_glu, ssm_b_glu, ssm_w_out, conv_w, conv_b, conv_w_out, w_o, ffn2_norm, ffn2_w_gate, ffn2_w_up, ffn2_w_down, final_norm):
    batch, seq, d = x.shape
    t = batch * seq
    chunks = seq // CHUNK

    bf = lambda w: w.astype(BF16)
    vec = lambda g: g.reshape(1, -1).astype(F32)

    later = (ffn2_w_gate, ffn2_w_up, ffn2_w_down, w_in, ssm_w_glu, ssm_w_out, conv_w_out, w_o)
    h1, wg2, wu2, wd2, w_in_b, w_glu_b, w_a_b, w_b_b, w_o_b = _ffn(
        x.reshape(t, d), vec(ffn1_norm), bf(ffn1_w_gate), bf(ffn1_w_up), bf(ffn1_w_down),
        cast=later)

    v_chunks, y_b, gates = _in_proj(h1, vec(mix_norm), w_in_b, conv_w.astype(F32), vec(conv_b),
                                    batch=batch, seq=seq)

    tables = _ssm_tables(ssm_lambda_re, ssm_lambda_im, ssm_log_dt, ssm_b_re, ssm_b_im,
                         ssm_c_re, ssm_c_im, ssm_d)
    g_planes = _ssm(v_chunks, *tables, chunks=chunks)

    h2 = _mix_out(g_planes, y_b, gates, h1, w_glu_b, vec(ssm_b_glu), w_a_b, w_b_b, w_o_b)

    out = _ffn(h2, vec(ffn2_norm), wg2, wu2, wd2, vec(final_norm))
    return out.reshape(batch, seq, d)
```

```python
import functools

import jax
import jax.numpy as jnp
from jax import lax
from jax.experimental import pallas as pl
from jax.experimental.pallas import tpu as pltpu

D_MODEL = 2048
SSM_WIDTH = 1024
SSM_GROUP = 16
SSM_GROUPS = 64
SSM_STATE = 64
CONV_WIDTH = 1024
CONV_K = 3
EPS = 1e-6

LANES = 128
SUBLANES = 8
BF16_SUBLANES = 16
MXU_DIM = 256
CHUNK = 16
PLANES = SSM_WIDTH // LANES
GROUPS_PER_PLANE = LANES // SSM_GROUP
HALF_STATE = GROUPS_PER_PLANE * SSM_STATE
PLANE_STATE = 2 * HALF_STATE
SCAN_STEPS_IN_GROUP = 3
SCAN_ROW0 = CHUNK + 1
CARRY_ROW0 = -(-(SCAN_ROW0 + SCAN_STEPS_IN_GROUP) // SUBLANES) * SUBLANES
APOW_ROWS = CARRY_ROW0 + SUBLANES
ACOL_IM = LANES // 2
VMEM_BYTES = 64 * 1024 * 1024
VMEM_LIMIT = VMEM_BYTES - 8 * 1024 * 1024
VMEM_LIMIT_FFN = VMEM_BYTES - 4 * 1024 * 1024
VMEM_LIMIT_IN_PROJ = VMEM_BYTES - 2 * 1024 * 1024

BF16 = jnp.bfloat16
F32 = jnp.float32
HIGHEST = lax.Precision.HIGHEST


def _rms(x, g):
    return x * lax.rsqrt(jnp.mean(x * x, axis=-1, keepdims=True) + EPS) * g


def _dot(a, b):
    return jnp.dot(a, b, preferred_element_type=F32)


def _dot_f32(a, b):
    return jnp.dot(a, b, precision=HIGHEST, preferred_element_type=F32)


def _split_bf16(a):
    hi = a.astype(BF16)
    return hi, (a - hi.astype(F32)).astype(BF16)


def _dot_split(a, b_parts):
    a_hi, a_lo = _split_bf16(a)
    b_hi, b_lo = b_parts
    return _dot(a_hi, b_hi) + (_dot(a_lo, b_hi) + _dot(a_hi, b_lo))


def _ffn_prologue(x_ref, gin_ref, xn_ref, o_ref):
    x = x_ref[...]
    xn_ref[...] = _rms(x, gin_ref[...]).astype(BF16)
    o_ref[...] = x


def _ffn_accumulate(xn_ref, o_ref, wg, wu, wd):
    xn = xn_ref[...]
    gate = _dot(xn, wg)
    up = _dot(xn, wu)
    act = ((0.5 * gate) * jax.nn.sigmoid(gate) * up).astype(BF16)
    o_ref[...] += _dot(act, wd)


def _last_hidden_width(d_ff, tf):
    return d_ff - (pl.cdiv(d_ff, tf) - 1) * tf


def _ffn_kernel(x_ref, gin_ref, wg_ref, wu_ref, wd_ref, *refs, final_norm, has_head, d_ff, n_cast):
    if final_norm:
        gout_ref, *refs = refs
    if has_head:
        head_ref, *refs = refs
    cast_src = refs[:n_cast]
    o_ref = refs[n_cast]
    cast_dst = refs[n_cast + 1:2 * n_cast + 1]
    xn_ref = refs[2 * n_cast + 1]
    i = pl.program_id(0)
    f = pl.program_id(1)
    nf = pl.num_programs(1)
    tf = wd_ref.shape[0]
    live = i > 0 if has_head else True

    if has_head:
        @pl.when(jnp.logical_and(i == 0, f == 0))
        def _():
            pltpu.sync_copy(head_ref, o_ref)

    @pl.when(jnp.logical_and(live, f == 0))
    def _():
        _ffn_prologue(x_ref, gin_ref, xn_ref, o_ref)

    def accumulate(width):
        _ffn_accumulate(xn_ref, o_ref, wg_ref[:, :width], wu_ref[:, :width], wd_ref[:width, :])

    last_width = _last_hidden_width(d_ff, tf)
    pl.when(jnp.logical_and(live, f < nf - 1))(functools.partial(accumulate, tf))
    pl.when(jnp.logical_and(live, f == nf - 1))(functools.partial(accumulate, last_width))

    if final_norm:
        @pl.when(f == nf - 1)
        def _():
            o_ref[...] = _rms(o_ref[...], gout_ref[...])

    for src, dst in zip(cast_src, cast_dst):
        dst[...] = src[...].astype(dst.dtype)


def _cast_block_rows(nrows, steps):
    br = BF16_SUBLANES
    while nrows % br or nrows // br > steps:
        br += BF16_SUBLANES
    return br


def _ffn(x, g_in, wg, wu, wd, g_out=None, *, head=None, cast=(), tm=1024, tf=512):
    t, d = x.shape
    d_ff = wg.shape[1]
    final_norm = g_out is not None
    has_head = head is not None
    nt = t // tm
    nf = pl.cdiv(d_ff, tf)
    steps = nt * nf
    row = pl.BlockSpec((tm, d), lambda i, f: (i, 0))
    vec = pl.BlockSpec((1, d), lambda i, f: (0, 0))
    if has_head:
        x_spec = pl.BlockSpec((tm, d), lambda i, f: (jnp.maximum(i, 1), 0))
        hidden = lambda i, f: jnp.where(i == 0, 0, f)
    else:
        x_spec = row
        hidden = lambda i, f: f
    in_specs = [x_spec, vec,
                pl.BlockSpec((d, tf), lambda i, f: (0, hidden(i, f))),
                pl.BlockSpec((d, tf), lambda i, f: (0, hidden(i, f))),
                pl.BlockSpec((tf, d), lambda i, f: (hidden(i, f), 0))]
    args = [x, g_in, wg, wu, wd]
    if final_norm:
        in_specs.append(vec)
        args.append(g_out)
    if has_head:
        in_specs.append(pl.BlockSpec(memory_space=pl.ANY))
        args.append(head)

    def cast_specs():
        specs = []
        for w in cast:
            nrows, ncols = w.shape
            br = _cast_block_rows(nrows, steps)
            specs.append(pl.BlockSpec(
                (br, ncols), lambda i, f, nb=nrows // br: (jnp.minimum(i * nf + f, nb - 1), 0)))
        return specs

    outs = pl.pallas_call(
        functools.partial(_ffn_kernel, final_norm=final_norm, has_head=has_head, d_ff=d_ff,
                          n_cast=len(cast)),
        grid=(nt, nf),
        in_specs=in_specs + cast_specs(),
        out_specs=[row] + cast_specs(),
        out_shape=[jax.ShapeDtypeStruct((t, d), F32)]
        + [jax.ShapeDtypeStruct(w.shape, BF16) for w in cast],
        scratch_shapes=[pltpu.VMEM((tm, d), BF16)],
        compiler_params=pltpu.CompilerParams(
            dimension_semantics=("arbitrary", "arbitrary"),
            vmem_limit_bytes=VMEM_LIMIT_FFN),
        name="ffn_final" if final_norm else "ffn_hidden",
    )(*args, *cast)
    return outs if cast else outs[0]


def _ffn_head_kernel(x_ref, gin_ref, wg_ref, wu_ref, wd_ref, o_ref, wgb_ref, wub_ref, wdb_ref,
                     xn_ref, *, d_ff):
    f = pl.program_id(0)
    nf = pl.num_programs(0)
    tf = wd_ref.shape[0]

    @pl.when(f == 0)
    def _():
        _ffn_prologue(x_ref, gin_ref, xn_ref, o_ref)

    def accumulate(width):
        wgb_ref[:, :width] = wg_ref[:, :width].astype(BF16)
        wub_ref[:, :width] = wu_ref[:, :width].astype(BF16)
        wdb_ref[:width, :] = wd_ref[:width, :].astype(BF16)
        _ffn_accumulate(xn_ref, o_ref, wgb_ref[:, :width], wub_ref[:, :width], wdb_ref[:width, :])

    last_width = _last_hidden_width(d_ff, tf)
    pl.when(f < nf - 1)(functools.partial(accumulate, tf))
    pl.when(f == nf - 1)(functools.partial(accumulate, last_width))


def _ffn_head(x, g_in, wg, wu, wd, *, tm=1024, tf=256):
    d = x.shape[1]
    d_ff = wg.shape[1]
    const = lambda f: (0, 0)
    cols = pl.BlockSpec((d, tf), lambda f: (0, f))
    rows = pl.BlockSpec((tf, d), lambda f: (f, 0))
    return pl.pallas_call(
        functools.partial(_ffn_head_kernel, d_ff=d_ff),
        grid=(pl.cdiv(d_ff, tf),),
        in_specs=[pl.BlockSpec((tm, d), const, pipeline_mode=pl.Buffered(1)),
                  pl.BlockSpec((1, d), const), cols, cols, rows],
        out_specs=[pl.BlockSpec((tm, d), const), cols, cols, rows],
        out_shape=[jax.ShapeDtypeStruct((tm, d), F32),
                   jax.ShapeDtypeStruct(wg.shape, BF16),
                   jax.ShapeDtypeStruct(wu.shape, BF16),
                   jax.ShapeDtypeStruct(wd.shape, BF16)],
        scratch_shapes=[pltpu.VMEM((tm, d), BF16)],
        compiler_params=pltpu.CompilerParams(
            dimension_semantics=("arbitrary",), vmem_limit_bytes=VMEM_LIMIT_FFN),
        name="ffn_head",
    )(x, g_in, wg, wu, wd)


GATE_GROUP0 = 2


def _store_planes(ref, val):
    for p in range(ref.shape[0]):
        ref[p] = val[:, p * LANES:(p + 1) * LANES]


def _load_planes(ref, p0, p1):
    return jnp.concatenate([ref[p] for p in range(p0, p1)], axis=1)


def _in_proj_kernel(h_ref, g_ref, w_ref, cw_ref, cb_ref, v_ref, yb_ref, gate_ref,
                    xn_ref, planes_ref, tail_ref, *, cblk):
    s = pl.program_id(1)
    j = pl.program_id(2)
    ts = h_ref.shape[0]
    nchunk = ts // CHUNK

    @pl.when(j == 0)
    def _ssm_input_and_b_gate():
        xn_ref[...] = _rms(h_ref[...], g_ref[...]).astype(BF16)
        xn = xn_ref[...]
        _store_planes(planes_ref, _dot(xn, w_ref[:, :SSM_WIDTH]))
        for p in range(PLANES):
            for l in range(CHUNK):
                piece = planes_ref[p, pl.ds(l, nchunk, stride=CHUNK), :]
                v_ref[p, :, l * LANES:(l + 1) * LANES] = piece.astype(v_ref.dtype)
        _store_planes(planes_ref, _dot(xn, w_ref[:, SSM_WIDTH:]))

    @pl.when(jnp.logical_and(j == 1, s == 0))
    def _():
        tail_ref[...] = jnp.zeros_like(tail_ref)

    @pl.when(j == 1)
    def _conv():
        xn = xn_ref[...]
        row = lax.broadcasted_iota(jnp.int32, (ts, 1), 0)
        for c in range(CONV_WIDTH // cblk):
            cs = slice(c * cblk, (c + 1) * cblk)
            vs = slice(CONV_WIDTH + c * cblk, CONV_WIDTH + (c + 1) * cblk)
            z = _dot(xn, w_ref[:, cs]) * _dot(xn, w_ref[:, vs])
            prev1 = tail_ref[SUBLANES - 1:SUBLANES, cs]
            prev2 = tail_ref[SUBLANES - 2:SUBLANES - 1, cs]
            z1 = jnp.where(row == 0, prev1, pltpu.roll(z, 1, axis=0))
            z2 = jnp.where(row == 0, prev2, jnp.where(row == 1, prev1, pltpu.roll(z, 2, axis=0)))
            conv = cb_ref[:, cs] + cw_ref[0:1, cs] * z2 + cw_ref[1:2, cs] * z1 + cw_ref[2:3, cs] * z
            b_gate = _load_planes(planes_ref, c * cblk // LANES, (c + 1) * cblk // LANES)
            yb_ref[:, cs] = (b_gate * conv).astype(yb_ref.dtype)
            tail_ref[:, cs] = z[ts - SUBLANES:, :]

    @pl.when(j >= GATE_GROUP0)
    def _gates():
        gate_ref[...] = jax.nn.sigmoid(_dot(xn_ref[...], w_ref[...])).astype(gate_ref.dtype)


def _in_proj(h, g, w, conv_w, conv_b, *, batch, seq, ts=1024, cblk=256):
    t, d = h.shape
    nseq = seq // ts
    gw = 2 * CONV_WIDTH
    ngroups = w.shape[1] // gw
    tok = lambda b, s, j: (b * nseq + s, 0)
    const = lambda b, s, j: (0, 0)
    return pl.pallas_call(
        functools.partial(_in_proj_kernel, cblk=cblk),
        grid=(batch, nseq, ngroups),
        in_specs=[pl.BlockSpec((ts, d), tok),
                  pl.BlockSpec((1, d), const),
                  pl.BlockSpec((d, gw), lambda b, s, j: (0, j)),
                  pl.BlockSpec((CONV_K, CONV_WIDTH), const),
                  pl.BlockSpec((1, CONV_WIDTH), const)],
        out_specs=(pl.BlockSpec((PLANES, ts // CHUNK, CHUNK * LANES),
                                lambda b, s, j: (0, b * nseq + s, 0)),
                   pl.BlockSpec((ts, CONV_WIDTH), tok),
                   pl.BlockSpec((ts, gw),
                                lambda b, s, j: (b * nseq + s, jnp.maximum(j - GATE_GROUP0, 0)))),
        out_shape=(jax.ShapeDtypeStruct((PLANES, t // CHUNK, CHUNK * LANES), BF16),
                   jax.ShapeDtypeStruct((t, CONV_WIDTH), BF16),
                   jax.ShapeDtypeStruct((t, 2 * d), BF16)),
        scratch_shapes=[pltpu.VMEM((ts, d), BF16),
                        pltpu.VMEM((PLANES, ts, LANES), F32),
                        pltpu.VMEM((SUBLANES, CONV_WIDTH), F32)],
        compiler_params=pltpu.CompilerParams(
            dimension_semantics=("arbitrary", "arbitrary", "arbitrary"),
            vmem_limit_bytes=VMEM_LIMIT_IN_PROJ),
        name="in_proj",
    )(h, g, w, conv_w, conv_b)


def _block_diag_tile(src, period, rows_per_group, cols_per_group):
    nrows = src.shape[0]
    ncols = cols_per_group * GROUPS_PER_PLANE
    k = lax.broadcasted_iota(jnp.int32, (src.shape[1], ncols), 0)
    c = lax.broadcasted_iota(jnp.int32, (src.shape[1], ncols), 1)
    tiled = _dot_f32(src, (k == c % period).astype(F32))
    r = lax.broadcasted_iota(jnp.int32, (nrows, ncols), 0)
    c = lax.broadcasted_iota(jnp.int32, (nrows, ncols), 1)
    return jnp.where(r // rows_per_group == c // cols_per_group, tiled, 0.0)


def _ssm_kernel(x_ref, btr_ref, bti_ref, ctr_ref, cti_ref, arow_ref, acol_ref, d_ref, o_ref,
                toep_ref, win_ref, wout_ref, y_ref, *, chunks):
    half = HALF_STATE

    @pl.when(pl.program_id(1) == 0)
    def _build_operators():
        btr = _block_diag_tile(btr_ref[...], SSM_STATE, SSM_GROUP, SSM_STATE)
        bti = _block_diag_tile(bti_ref[...], SSM_STATE, SSM_GROUP, SSM_STATE)
        ctr = _block_diag_tile(ctr_ref[...], SSM_GROUP, SSM_STATE, SSM_GROUP)
        cti = _block_diag_tile(cti_ref[...], SSM_GROUP, SSM_STATE, SSM_GROUP)
        ctr_parts = _split_bf16(ctr)
        cti_parts = _split_bf16(cti)
        zero = jnp.zeros((LANES, LANES), BF16)
        for k in range(CHUNK):
            ar = arow_ref[k:k + 1, :half]
            ai = arow_ref[k:k + 1, half:]
            wr = btr * ar - bti * ai
            wi = btr * ai + bti * ar
            rows = slice((CHUNK - 1 - k) * LANES, (CHUNK - k) * LANES)
            win_ref[rows, :half] = wr.astype(BF16)
            win_ref[rows, half:] = wi.astype(BF16)
            lag = (_dot_split(wr, ctr_parts) - _dot_split(wi, cti_parts)).astype(BF16)
            for lin in range(CHUNK - k):
                lout = lin + k
                toep_ref[lin * LANES:(lin + 1) * LANES, lout * LANES:(lout + 1) * LANES] = lag
        for lin in range(1, CHUNK, MXU_DIM // LANES):
            toep_ref[lin * LANES:(lin + 1) * LANES, (lin - 1) * LANES:lin * LANES] = zero
        for l in range(CHUNK):
            arc = acol_ref[:, l + 1:l + 2]
            aic = acol_ref[:, ACOL_IM + l + 1:ACOL_IM + l + 2]
            cols = slice(l * LANES, (l + 1) * LANES)
            wout_ref[:half, cols] = (ctr * arc - cti * aic).astype(BF16)
            wout_ref[half:, cols] = (-(ctr * aic + cti * arc)).astype(BF16)

    rows, width = x_ref.shape
    s_in = _dot(x_ref[...], win_ref[...])
    sr = s_in[:, :half]
    si = s_in[:, half:]
    for cb in range(width // MXU_DIM):
        cols = slice(cb * MXU_DIM, (cb + 1) * MXU_DIM)
        kk = (cb + 1) * MXU_DIM
        y_ref[:, cols] = (_dot(x_ref[:, :kk], toep_ref[:kk, cols])
                          + d_ref[:, cols] * x_ref[:, cols].astype(F32))
    cidx = lax.broadcasted_iota(jnp.int32, (rows, 1), 0) % chunks
    for k in range(SCAN_STEPS_IN_GROUP):
        sh = 1 << k
        ar = arow_ref[SCAN_ROW0 + k:SCAN_ROW0 + k + 1, :half]
        ai = arow_ref[SCAN_ROW0 + k:SCAN_ROW0 + k + 1, half:]
        keep = cidx % SUBLANES >= sh
        pr = jnp.where(keep, pltpu.roll(sr, sh, axis=0), 0.0)
        pi = jnp.where(keep, pltpu.roll(si, sh, axis=0), 0.0)
        sr, si = sr + (ar * pr - ai * pi), si + (ar * pi + ai * pr)
    cr = arow_ref[CARRY_ROW0:CARRY_ROW0 + SUBLANES, :half]
    ci = arow_ref[CARRY_ROW0:CARRY_ROW0 + SUBLANES, half:]
    out_r, out_i = [], []
    for g in range(rows // SUBLANES):
        gr = sr[g * SUBLANES:(g + 1) * SUBLANES, :]
        gi = si[g * SUBLANES:(g + 1) * SUBLANES, :]
        if g % (chunks // SUBLANES):
            lr = jnp.broadcast_to(out_r[-1][SUBLANES - 1:, :], gr.shape)
            li = jnp.broadcast_to(out_i[-1][SUBLANES - 1:, :], gi.shape)
            gr, gi = gr + (cr * lr - ci * li), gi + (cr * li + ci * lr)
        out_r.append(gr)
        out_i.append(gi)
    sr = jnp.concatenate(out_r, axis=0)
    si = jnp.concatenate(out_i, axis=0)
    first = cidx >= 1
    pr = jnp.where(first, pltpu.roll(sr, 1, axis=0), 0.0).astype(BF16)
    pi = jnp.where(first, pltpu.roll(si, 1, axis=0), 0.0).astype(BF16)
    s_prev = jnp.concatenate([pr, pi], axis=1)
    for cb in range(width // MXU_DIM):
        cols = slice(cb * MXU_DIM, (cb + 1) * MXU_DIM)
        y = y_ref[:, cols] + _dot(s_prev, wout_ref[:, cols])
        o_ref[:, cols] = jax.nn.gelu(y).astype(o_ref.dtype)


def _ssm(xc, btr, bti, ctr, cti, arow, acol, d_tiled, *, chunks, rows=1024):
    planes, nrows, width = xc.shape
    nh = nrows // rows

    def plane(*shape):
        return pl.BlockSpec((None,) + shape, lambda j, h: (j,) + (0,) * len(shape))

    return pl.pallas_call(
        functools.partial(_ssm_kernel, chunks=chunks),
        grid=(planes, nh),
        in_specs=[
            pl.BlockSpec((None, rows, width), lambda j, h: (j, h, 0)),
            plane(LANES, LANES), plane(LANES, LANES),
            plane(HALF_STATE, LANES), plane(HALF_STATE, LANES),
            plane(APOW_ROWS, PLANE_STATE), plane(HALF_STATE, LANES),
            plane(1, width),
        ],
        out_specs=pl.BlockSpec((None, rows, width), lambda j, h: (j, h, 0)),
        out_shape=jax.ShapeDtypeStruct((planes, nrows, width), BF16),
        scratch_shapes=[pltpu.VMEM((width, width), BF16),
                        pltpu.VMEM((width, PLANE_STATE), BF16),
                        pltpu.VMEM((PLANE_STATE, width), BF16),
                        pltpu.VMEM((rows, width), F32)],
        compiler_params=pltpu.CompilerParams(
            dimension_semantics=("arbitrary", "arbitrary"), vmem_limit_bytes=VMEM_LIMIT),
        name="ssm_scan",
    )(xc, btr, bti, ctr, cti, arow, acol, d_tiled)


def _ssm_tables(lam_re, lam_im, log_dt, b_re, b_im, c_re, c_im, d_skip):
    lam_re = jnp.minimum(lam_re, -1e-4)
    dt = jnp.exp(log_dt)[:, None]
    mag = jnp.exp(lam_re * dt)
    a_re = mag * jnp.cos(lam_im * dt)
    a_im = mag * jnp.sin(lam_im * dt)
    den = lam_re * lam_re + lam_im * lam_im
    p = a_re - 1.0
    f_re = ((p * lam_re + a_im * lam_im) / den)[:, :, None]
    f_im = ((a_im * lam_re - p * lam_im) / den)[:, :, None]
    bb_re = f_re * b_re - f_im * b_im
    bb_im = f_re * b_im + f_im * b_re

    N, C, P, L = SSM_STATE, SSM_GROUP, PLANES, CHUNK

    def bt(bb):
        m = jnp.transpose(bb, (0, 2, 1)).reshape(P, LANES, N)
        return jnp.pad(m, ((0, 0), (0, 0), (0, LANES - N)))

    def ct(c):
        m = jnp.transpose(c, (0, 2, 1)).reshape(P, HALF_STATE, C)
        return jnp.pad(m, ((0, 0), (0, 0), (0, LANES - C)))

    ks = (list(range(L + 1)) + [L << m for m in range(SCAN_STEPS_IN_GROUP)])
    ks += [0] * (CARRY_ROW0 - len(ks)) + [L * (r + 1) for r in range(SUBLANES)]
    ks = jnp.asarray(ks, F32).reshape(-1, 1, 1)
    m = jnp.exp(ks * (lam_re * dt))
    pr = m * jnp.cos(ks * (lam_im * dt))
    pi = m * jnp.sin(ks * (lam_im * dt))
    arow = jnp.concatenate([pr.reshape(APOW_ROWS, P, HALF_STATE),
                            pi.reshape(APOW_ROWS, P, HALF_STATE)], axis=2)
    arow = jnp.transpose(arow, (1, 0, 2))

    def col(pw):
        c = jnp.transpose(pw[:L + 1].reshape(L + 1, P, HALF_STATE), (1, 2, 0))
        return jnp.pad(c, ((0, 0), (0, 0), (0, ACOL_IM - (L + 1))))

    acol = jnp.concatenate([col(pr), col(pi)], axis=2)
    d_tiled = jnp.tile(d_skip.reshape(P, 1, LANES), (1, 1, L))
    return bt(bb_re), bt(bb_im), ct(c_re), ct(c_im), arow, acol, d_tiled


def _mix_out_kernel(g_ref, yb_ref, gate_ref, h_ref, wglu_ref, bglu_ref, wa_ref, wb_ref, wo_ref, o_ref,
                    gs_ref):
    nchunk = g_ref.shape[1]
    for p in range(PLANES):
        for l in range(CHUNK):
            piece = g_ref[p, :, l * LANES:(l + 1) * LANES].astype(F32)
            gs_ref[p, pl.ds(l, nchunk, stride=CHUNK), :] = piece
    g = _load_planes(gs_ref, 0, PLANES)
    glu = _dot(g.astype(BF16), wglu_ref[...]) + bglu_ref[...]
    y_a = (g * jax.nn.sigmoid(glu)).astype(BF16)
    z_a = _dot(y_a, wa_ref[...])
    z_b = _dot(yb_ref[...], wb_ref[...])
    merged = (gate_ref[:, :D_MODEL].astype(F32) * z_a
              + gate_ref[:, D_MODEL:].astype(F32) * z_b).astype(BF16)
    o_ref[...] = h_ref[...] + _dot(merged, wo_ref[...])


def _mix_out(g_planes, y_b, gates, h, w_glu, b_glu, w_a, w_b, w_o, *, tm=512):
    t, d = h.shape

    def const(shape):
        return pl.BlockSpec(shape, lambda i: (0,) * len(shape), pipeline_mode=pl.Buffered(1))

    return pl.pallas_call(
        _mix_out_kernel,
        grid=(t // tm,),
        in_specs=[
            pl.BlockSpec((PLANES, tm // CHUNK, CHUNK * LANES), lambda i: (0, i, 0)),
            pl.BlockSpec((tm, CONV_WIDTH), lambda i: (i, 0)),
            pl.BlockSpec((tm, 2 * d), lambda i: (i, 0)),
            pl.BlockSpec((tm, d), lambda i: (i, 0)),
            const((SSM_WIDTH, SSM_WIDTH)), const((1, SSM_WIDTH)),
            const((SSM_WIDTH, d)), const((CONV_WIDTH, d)), const((d, d)),
        ],
        out_specs=pl.BlockSpec((tm, d), lambda i: (i, 0)),
        out_shape=jax.ShapeDtypeStruct((t, d), F32),
        scratch_shapes=[pltpu.VMEM((PLANES, tm, LANES), F32)],
        compiler_params=pltpu.CompilerParams(
            dimension_semantics=("parallel",), vmem_limit_bytes=VMEM_LIMIT),
        name="mix_out",
    )(g_planes, y_b, gates, h, w_glu, b_glu, w_a, w_b, w_o)


def kernel(x, ffn1_norm, ffn1_w_gate, ffn1_w_up, ffn1_w_down, mix_norm, w_in, ssm_lambda_re, ssm_lambda_im, ssm_log_dt, ssm_b_re, ssm_b_im, ssm_c_re, ssm_c_im, ssm_d, ssm_w_glu, ssm_b_glu, ssm_w_out, conv_w, conv_b, conv_w_out, w_o, ffn2_norm, ffn2_w_gate, ffn2_w_up, ffn2_w_down, final_norm):
    batch, seq, d = x.shape
    t = batch * seq
    chunks = seq // CHUNK

    vec = lambda g: g.reshape(1, -1).astype(F32)

    later = (ffn2_w_gate, ffn2_w_up, ffn2_w_down, w_in, ssm_w_glu, ssm_w_out, conv_w_out, w_o)
    x2 = x.reshape(t, d)
    h1_head, wg1, wu1, wd1 = _ffn_head(x2, vec(ffn1_norm), ffn1_w_gate, ffn1_w_up, ffn1_w_down)
    h1, wg2, wu2, wd2, w_in_b, w_glu_b, w_a_b, w_b_b, w_o_b = _ffn(
        x2, vec(ffn1_norm), wg1, wu1, wd1, head=h1_head, cast=later)

    v_chunks, y_b, gates = _in_proj(h1, vec(mix_norm), w_in_b, conv_w.astype(F32), vec(conv_b),
                                    batch=batch, seq=seq)

    tables = _ssm_tables(ssm_lambda_re, ssm_lambda_im, ssm_log_dt, ssm_b_re, ssm_b_im,
                         ssm_c_re, ssm_c_im, ssm_d)
    g_planes = _ssm(v_chunks, *tables, chunks=chunks)

    h2 = _mix_out(g_planes, y_b, gates, h1, w_glu_b, vec(ssm_b_glu), w_a_b, w_b_b, w_o_b)

    out = _ffn(h2, vec(ffn2_norm), wg2, wu2, wd2, vec(final_norm))
    return out.reshape(batch, seq, d)
```

```python
import functools

import jax
import jax.numpy as jnp
from jax import lax
from jax.experimental import pallas as pl
from jax.experimental.pallas import tpu as pltpu

D_MODEL = 2048
SSM_WIDTH = 1024
SSM_GROUP = 16
SSM_STATE = 64
CONV_WIDTH = 1024
CONV_K = 3
EPS = 1e-6

LANES = 128
SUBLANES = 8
BF16_SUBLANES = 16
MXU_DIM = 256
CHUNK = 16
PLANES = SSM_WIDTH // LANES
GROUPS_PER_PLANE = LANES // SSM_GROUP
HALF_STATE = GROUPS_PER_PLANE * SSM_STATE
PLANE_STATE = 2 * HALF_STATE
SCAN_STEPS_IN_GROUP = 3
SCAN_ROW0 = CHUNK + 1
CARRY_ROW0 = -(-(SCAN_ROW0 + SCAN_STEPS_IN_GROUP) // SUBLANES) * SUBLANES
APOW_ROWS = CARRY_ROW0 + SUBLANES
ACOL_IM = LANES // 2
VMEM_BYTES = 64 * 1024 * 1024
VMEM_LIMIT = VMEM_BYTES - 8 * 1024 * 1024
VMEM_LIMIT_FFN = VMEM_BYTES - 4 * 1024 * 1024
VMEM_LIMIT_IN_PROJ = VMEM_BYTES - 2 * 1024 * 1024

BF16 = jnp.bfloat16
F32 = jnp.float32
HIGHEST = lax.Precision.HIGHEST


def _rms(x, g):
    return x * lax.rsqrt(jnp.mean(x * x, axis=-1, keepdims=True) + EPS) * g


def _dot(a, b):
    return jnp.dot(a, b, preferred_element_type=F32)


def _dot_f32(a, b):
    return jnp.dot(a, b, precision=HIGHEST, preferred_element_type=F32)


def _split_bf16(a):
    hi = a.astype(BF16)
    return hi, (a - hi.astype(F32)).astype(BF16)


def _dot_split(a, b_parts):
    a_hi, a_lo = _split_bf16(a)
    b_hi, b_lo = b_parts
    return _dot(a_hi, b_hi) + (_dot(a_lo, b_hi) + _dot(a_hi, b_lo))


def _ffn_prologue(x_ref, gin_ref, xn_ref, o_ref):
    x = x_ref[...]
    xn_ref[...] = _rms(x, gin_ref[...]).astype(BF16)
    o_ref[...] = x


def _ffn_accumulate(xn_ref, o_ref, wg, wu, wd):
    xn = xn_ref[...]
    gate = _dot(xn, wg)
    up = _dot(xn, wu)
    act = ((0.5 * gate) * jax.nn.sigmoid(gate) * up).astype(BF16)
    o_ref[...] += _dot(act, wd)


def _last_hidden_width(d_ff, tf):
    return d_ff - (pl.cdiv(d_ff, tf) - 1) * tf


def _cast_block_rows(nrows, steps):
    br = BF16_SUBLANES
    while nrows % br or nrows // br > steps:
        br += BF16_SUBLANES
    return br


def _ffn_stream_kernel(x_hbm, gin_ref, wg_ref, wu_ref, wd_ref, *refs, final_norm, has_head,
                       d_ff, n_cast):
    if final_norm:
        gout_ref, *refs = refs
    if has_head:
        head_hbm, *refs = refs
    cast_src = refs[:n_cast]
    o_hbm = refs[n_cast]
    cast_dst = refs[n_cast + 1:2 * n_cast + 1]
    xn_ref, acc_ref, in_sem, out_sem, *head_sem = refs[2 * n_cast + 1:]
    i = pl.program_id(0)
    f = pl.program_id(1)
    nt = pl.num_programs(0)
    nf = pl.num_programs(1)
    tm = xn_ref.shape[0]
    tf = wd_ref.shape[0]
    first = 1 if has_head else 0
    live = i >= first
    slot = i % 2
    other = 1 - slot
    acc = acc_ref.at[slot]

    def tile_rows(ref, tile):
        return ref.at[pl.ds(pl.multiple_of(tile * tm, tm), tm)]

    def x_copy(tile, s):
        return pltpu.make_async_copy(tile_rows(x_hbm, tile), acc_ref.at[s], in_sem.at[s])

    def o_copy(tile, s):
        return pltpu.make_async_copy(acc_ref.at[s], tile_rows(o_hbm, tile), out_sem.at[s])

    @pl.when(jnp.logical_and(i == 0, f == 0))
    def _():
        x_copy(first, first % 2).start()
        if has_head:
            head_copy = pltpu.make_async_copy(head_hbm, tile_rows(o_hbm, 0), head_sem[0].at[0])
            head_copy.start()
            head_copy.wait()

    @pl.when(jnp.logical_and(live, f == 0))
    def _():
        x_copy(i, slot).wait()
        xn_ref[...] = _rms(acc[...], gin_ref[...]).astype(BF16)

    @pl.when(jnp.logical_and(live, f == 1))
    def _():
        @pl.when(i > first)
        def _():
            o_copy(i - 1, other).wait()

        @pl.when(i + 1 < nt)
        def _():
            x_copy(i + 1, other).start()

    def accumulate(width):
        _ffn_accumulate(xn_ref, acc, wg_ref[:, :width], wu_ref[:, :width], wd_ref[:width, :])

    last_width = _last_hidden_width(d_ff, tf)
    pl.when(jnp.logical_and(live, f < nf - 1))(functools.partial(accumulate, tf))
    pl.when(jnp.logical_and(live, f == nf - 1))(functools.partial(accumulate, last_width))

    @pl.when(jnp.logical_and(live, f == nf - 1))
    def _():
        if final_norm:
            acc[...] = _rms(acc[...], gout_ref[...])
        o_copy(i, slot).start()

        @pl.when(i == nt - 1)
        def _():
            o_copy(i, slot).wait()

    for src, dst in zip(cast_src, cast_dst):
        dst[...] = src[...].astype(dst.dtype)


def _ffn_stream(x, g_in, wg, wu, wd, g_out=None, *, head=None, cast=(), tm=1024, tf):
    t, d = x.shape
    d_ff = wg.shape[1]
    final_norm = g_out is not None
    has_head = head is not None
    nt = t // tm
    nf = pl.cdiv(d_ff, tf)
    assert nf >= 2, "step 1 hands the free accumulator slot to the next tile's input"
    steps = nt * nf
    vec = pl.BlockSpec((1, d), lambda i, f: (0, 0))
    any_space = pl.BlockSpec(memory_space=pl.ANY)
    hidden = (lambda i, f: jnp.where(i == 0, 0, f)) if has_head else (lambda i, f: f)
    in_specs = [any_space, vec,
                pl.BlockSpec((d, tf), lambda i, f: (0, hidden(i, f))),
                pl.BlockSpec((d, tf), lambda i, f: (0, hidden(i, f))),
                pl.BlockSpec((tf, d), lambda i, f: (hidden(i, f), 0))]
    args = [x, g_in, wg, wu, wd]
    if final_norm:
        in_specs.append(vec)
        args.append(g_out)
    if has_head:
        in_specs.append(any_space)
        args.append(head)

    def cast_specs():
        specs = []
        for w in cast:
            nrows, ncols = w.shape
            br = _cast_block_rows(nrows, steps)
            specs.append(pl.BlockSpec(
                (br, ncols), lambda i, f, nb=nrows // br: (jnp.minimum(i * nf + f, nb - 1), 0)))
        return specs

    dma_sems = pltpu.SemaphoreType.DMA
    outs = pl.pallas_call(
        functools.partial(_ffn_stream_kernel, final_norm=final_norm, has_head=has_head,
                          d_ff=d_ff, n_cast=len(cast)),
        grid=(nt, nf),
        in_specs=in_specs + cast_specs(),
        out_specs=[any_space] + cast_specs(),
        out_shape=[jax.ShapeDtypeStruct((t, d), F32)]
        + [jax.ShapeDtypeStruct(w.shape, BF16) for w in cast],
        scratch_shapes=[pltpu.VMEM((tm, d), BF16), pltpu.VMEM((2, tm, d), F32),
                        dma_sems((2,)), dma_sems((2,))] + ([dma_sems((1,))] if has_head else []),
        compiler_params=pltpu.CompilerParams(
            dimension_semantics=("arbitrary", "arbitrary"),
            vmem_limit_bytes=VMEM_LIMIT_FFN),
        name="ffn_final" if final_norm else "ffn_hidden",
    )(*args, *cast)
    return outs if cast else outs[0]


def _ffn_head_kernel(x_ref, gin_ref, wg_ref, wu_ref, wd_ref, o_ref, wgb_ref, wub_ref, wdb_ref,
                     xn_ref, *, d_ff):
    f = pl.program_id(0)
    nf = pl.num_programs(0)
    tf = wd_ref.shape[0]

    @pl.when(f == 0)
    def _():
        _ffn_prologue(x_ref, gin_ref, xn_ref, o_ref)

    def accumulate(width):
        wgb_ref[:, :width] = wg_ref[:, :width].astype(BF16)
        wub_ref[:, :width] = wu_ref[:, :width].astype(BF16)
        wdb_ref[:width, :] = wd_ref[:width, :].astype(BF16)
        _ffn_accumulate(xn_ref, o_ref, wgb_ref[:, :width], wub_ref[:, :width], wdb_ref[:width, :])

    last_width = _last_hidden_width(d_ff, tf)
    pl.when(f < nf - 1)(functools.partial(accumulate, tf))
    pl.when(f == nf - 1)(functools.partial(accumulate, last_width))


def _ffn_head(x, g_in, wg, wu, wd, *, tm=1024, tf=256):
    d = x.shape[1]
    d_ff = wg.shape[1]
    const = lambda f: (0, 0)
    cols = pl.BlockSpec((d, tf), lambda f: (0, f))
    rows = pl.BlockSpec((tf, d), lambda f: (f, 0))
    return pl.pallas_call(
        functools.partial(_ffn_head_kernel, d_ff=d_ff),
        grid=(pl.cdiv(d_ff, tf),),
        in_specs=[pl.BlockSpec((tm, d), const, pipeline_mode=pl.Buffered(1)),
                  pl.BlockSpec((1, d), const), cols, cols, rows],
        out_specs=[pl.BlockSpec((tm, d), const), cols, cols, rows],
        out_shape=[jax.ShapeDtypeStruct((tm, d), F32),
                   jax.ShapeDtypeStruct(wg.shape, BF16),
                   jax.ShapeDtypeStruct(wu.shape, BF16),
                   jax.ShapeDtypeStruct(wd.shape, BF16)],
        scratch_shapes=[pltpu.VMEM((tm, d), BF16)],
        compiler_params=pltpu.CompilerParams(
            dimension_semantics=("arbitrary",), vmem_limit_bytes=VMEM_LIMIT_FFN),
        name="ffn_head",
    )(x, g_in, wg, wu, wd)


GATE_GROUP0 = 2


def _store_planes(ref, val):
    for p in range(ref.shape[0]):
        ref[p] = val[:, p * LANES:(p + 1) * LANES]


def _load_planes(ref, p0, p1):
    return jnp.concatenate([ref[p] for p in range(p0, p1)], axis=1)


def _in_proj_kernel(h_ref, g_ref, w_ref, cw_ref, cb_ref, v_ref, yb_ref, gate_ref,
                    xn_ref, planes_ref, tail_ref, *, cblk):
    s = pl.program_id(1)
    j = pl.program_id(2)
    ts = h_ref.shape[0]
    nchunk = ts // CHUNK

    @pl.when(j == 0)
    def _ssm_input_and_b_gate():
        xn_ref[...] = _rms(h_ref[...], g_ref[...]).astype(BF16)
        xn = xn_ref[...]
        _store_planes(planes_ref, _dot(xn, w_ref[:, :SSM_WIDTH]))
        for p in range(PLANES):
            for l in range(CHUNK):
                piece = planes_ref[p, pl.ds(l, nchunk, stride=CHUNK), :]
                v_ref[p, :, l * LANES:(l + 1) * LANES] = piece.astype(v_ref.dtype)
        _store_planes(planes_ref, _dot(xn, w_ref[:, SSM_WIDTH:]))

    @pl.when(jnp.logical_and(j == 1, s == 0))
    def _():
        tail_ref[...] = jnp.zeros_like(tail_ref)

    @pl.when(j == 1)
    def _conv():
        xn = xn_ref[...]
        row = lax.broadcasted_iota(jnp.int32, (ts, 1), 0)
        for c in range(CONV_WIDTH // cblk):
            cs = slice(c * cblk, (c + 1) * cblk)
            vs = slice(CONV_WIDTH + c * cblk, CONV_WIDTH + (c + 1) * cblk)
            z = _dot(xn, w_ref[:, cs]) * _dot(xn, w_ref[:, vs])
            prev1 = tail_ref[SUBLANES - 1:SUBLANES, cs]
            prev2 = tail_ref[SUBLANES - 2:SUBLANES - 1, cs]
            z1 = jnp.where(row == 0, prev1, pltpu.roll(z, 1, axis=0))
            z2 = jnp.where(row == 0, prev2, jnp.where(row == 1, prev1, pltpu.roll(z, 2, axis=0)))
            conv = cb_ref[:, cs] + cw_ref[0:1, cs] * z2 + cw_ref[1:2, cs] * z1 + cw_ref[2:3, cs] * z
            b_gate = _load_planes(planes_ref, c * cblk // LANES, (c + 1) * cblk // LANES)
            yb_ref[:, cs] = (b_gate * conv).astype(yb_ref.dtype)
            tail_ref[:, cs] = z[ts - SUBLANES:, :]

    @pl.when(j >= GATE_GROUP0)
    def _gates():
        gate_ref[...] = jax.nn.sigmoid(_dot(xn_ref[...], w_ref[...])).astype(gate_ref.dtype)


def _in_proj(h, g, w, conv_w, conv_b, *, batch, seq, ts=1024, cblk=256):
    t, d = h.shape
    nseq = seq // ts
    gw = 2 * CONV_WIDTH
    ngroups = w.shape[1] // gw
    tok = lambda b, s, j: (b * nseq + s, 0)
    const = lambda b, s, j: (0, 0)
    return pl.pallas_call(
        functools.partial(_in_proj_kernel, cblk=cblk),
        grid=(batch, nseq, ngroups),
        in_specs=[pl.BlockSpec((ts, d), tok),
                  pl.BlockSpec((1, d), const),
                  pl.BlockSpec((d, gw), lambda b, s, j: (0, j)),
                  pl.BlockSpec((CONV_K, CONV_WIDTH), const),
                  pl.BlockSpec((1, CONV_WIDTH), const)],
        out_specs=(pl.BlockSpec((PLANES, ts // CHUNK, CHUNK * LANES),
                                lambda b, s, j: (0, b * nseq + s, 0)),
                   pl.BlockSpec((ts, CONV_WIDTH), tok),
                   pl.BlockSpec((ts, gw),
                                lambda b, s, j: (b * nseq + s, jnp.maximum(j - GATE_GROUP0, 0)))),
        out_shape=(jax.ShapeDtypeStruct((PLANES, t // CHUNK, CHUNK * LANES), BF16),
                   jax.ShapeDtypeStruct((t, CONV_WIDTH), BF16),
                   jax.ShapeDtypeStruct((t, 2 * d), BF16)),
        scratch_shapes=[pltpu.VMEM((ts, d), BF16),
                        pltpu.VMEM((PLANES, ts, LANES), F32),
                        pltpu.VMEM((SUBLANES, CONV_WIDTH), F32)],
        compiler_params=pltpu.CompilerParams(
            dimension_semantics=("arbitrary", "arbitrary", "arbitrary"),
            vmem_limit_bytes=VMEM_LIMIT_IN_PROJ),
        name="in_proj",
    )(h, g, w, conv_w, conv_b)


def _block_diag_tile(src, period, rows_per_group, cols_per_group):
    nrows = src.shape[0]
    ncols = cols_per_group * GROUPS_PER_PLANE
    k = lax.broadcasted_iota(jnp.int32, (src.shape[1], ncols), 0)
    c = lax.broadcasted_iota(jnp.int32, (src.shape[1], ncols), 1)
    tiled = _dot_f32(src, (k == c % period).astype(F32))
    r = lax.broadcasted_iota(jnp.int32, (nrows, ncols), 0)
    c = lax.broadcasted_iota(jnp.int32, (nrows, ncols), 1)
    return jnp.where(r // rows_per_group == c // cols_per_group, tiled, 0.0)


def _ssm_kernel(x_ref, btr_ref, bti_ref, ctr_ref, cti_ref, arow_ref, acol_ref, d_ref, o_ref,
                toep_ref, win_ref, wout_ref, y_ref, *, chunks):
    half = HALF_STATE

    @pl.when(pl.program_id(1) == 0)
    def _build_operators():
        btr = _block_diag_tile(btr_ref[...], SSM_STATE, SSM_GROUP, SSM_STATE)
        bti = _block_diag_tile(bti_ref[...], SSM_STATE, SSM_GROUP, SSM_STATE)
        ctr = _block_diag_tile(ctr_ref[...], SSM_GROUP, SSM_STATE, SSM_GROUP)
        cti = _block_diag_tile(cti_ref[...], SSM_GROUP, SSM_STATE, SSM_GROUP)
        ctr_parts = _split_bf16(ctr)
        cti_parts = _split_bf16(cti)
        zero = jnp.zeros((LANES, LANES), BF16)
        for k in range(CHUNK):
            ar = arow_ref[k:k + 1, :half]
            ai = arow_ref[k:k + 1, half:]
            wr = btr * ar - bti * ai
            wi = btr * ai + bti * ar
            rows = slice((CHUNK - 1 - k) * LANES, (CHUNK - k) * LANES)
            win_ref[rows, :half] = wr.astype(BF16)
            win_ref[rows, half:] = wi.astype(BF16)
            lag = (_dot_split(wr, ctr_parts) - _dot_split(wi, cti_parts)).astype(BF16)
            for lin in range(CHUNK - k):
                lout = lin + k
                toep_ref[lin * LANES:(lin + 1) * LANES, lout * LANES:(lout + 1) * LANES] = lag
        for lin in range(1, CHUNK, MXU_DIM // LANES):
            toep_ref[lin * LANES:(lin + 1) * LANES, (lin - 1) * LANES:lin * LANES] = zero
        for l in range(CHUNK):
            arc = acol_ref[:, l + 1:l + 2]
            aic = acol_ref[:, ACOL_IM + l + 1:ACOL_IM + l + 2]
            cols = slice(l * LANES, (l + 1) * LANES)
            wout_ref[:half, cols] = (ctr * arc - cti * aic).astype(BF16)
            wout_ref[half:, cols] = (-(ctr * aic + cti * arc)).astype(BF16)

    rows, width = x_ref.shape
    s_in = _dot(x_ref[...], win_ref[...])
    sr = s_in[:, :half]
    si = s_in[:, half:]
    for cb in range(width // MXU_DIM):
        cols = slice(cb * MXU_DIM, (cb + 1) * MXU_DIM)
        kk = (cb + 1) * MXU_DIM
        y_ref[:, cols] = (_dot(x_ref[:, :kk], toep_ref[:kk, cols])
                          + d_ref[:, cols] * x_ref[:, cols].astype(F32))
    cidx = lax.broadcasted_iota(jnp.int32, (rows, 1), 0) % chunks
    for k in range(SCAN_STEPS_IN_GROUP):
        sh = 1 << k
        ar = arow_ref[SCAN_ROW0 + k:SCAN_ROW0 + k + 1, :half]
        ai = arow_ref[SCAN_ROW0 + k:SCAN_ROW0 + k + 1, half:]
        keep = cidx % SUBLANES >= sh
        pr = jnp.where(keep, pltpu.roll(sr, sh, axis=0), 0.0)
        pi = jnp.where(keep, pltpu.roll(si, sh, axis=0), 0.0)
        sr, si = sr + (ar * pr - ai * pi), si + (ar * pi + ai * pr)
    cr = arow_ref[CARRY_ROW0:CARRY_ROW0 + SUBLANES, :half]
    ci = arow_ref[CARRY_ROW0:CARRY_ROW0 + SUBLANES, half:]
    out_r, out_i = [], []
    for g in range(rows // SUBLANES):
        gr = sr[g * SUBLANES:(g + 1) * SUBLANES, :]
        gi = si[g * SUBLANES:(g + 1) * SUBLANES, :]
        if g % (chunks // SUBLANES):
            lr = jnp.broadcast_to(out_r[-1][SUBLANES - 1:, :], gr.shape)
            li = jnp.broadcast_to(out_i[-1][SUBLANES - 1:, :], gi.shape)
            gr, gi = gr + (cr * lr - ci * li), gi + (cr * li + ci * lr)
        out_r.append(gr)
        out_i.append(gi)
    sr = jnp.concatenate(out_r, axis=0)
    si = jnp.concatenate(out_i, axis=0)
    first = cidx >= 1
    pr = jnp.where(first, pltpu.roll(sr, 1, axis=0), 0.0).astype(BF16)
    pi = jnp.where(first, pltpu.roll(si, 1, axis=0), 0.0).astype(BF16)
    s_prev = jnp.concatenate([pr, pi], axis=1)
    for cb in range(width // MXU_DIM):
        cols = slice(cb * MXU_DIM, (cb + 1) * MXU_DIM)
        y = y_ref[:, cols] + _dot(s_prev, wout_ref[:, cols])
        o_ref[:, cols] = jax.nn.gelu(y).astype(o_ref.dtype)


def _ssm(xc, btr, bti, ctr, cti, arow, acol, d_tiled, *, chunks, rows=1024):
    planes, nrows, width = xc.shape
    nh = nrows // rows

    def plane(*shape):
        return pl.BlockSpec((None,) + shape, lambda j, h: (j,) + (0,) * len(shape))

    return pl.pallas_call(
        functools.partial(_ssm_kernel, chunks=chunks),
        grid=(planes, nh),
        in_specs=[
            pl.BlockSpec((None, rows, width), lambda j, h: (j, h, 0)),
            plane(LANES, LANES), plane(LANES, LANES),
            plane(HALF_STATE, LANES), plane(HALF_STATE, LANES),
            plane(APOW_ROWS, PLANE_STATE), plane(HALF_STATE, LANES),
            plane(1, width),
        ],
        out_specs=pl.BlockSpec((None, rows, width), lambda j, h: (j, h, 0)),
        out_shape=jax.ShapeDtypeStruct((planes, nrows, width), BF16),
        scratch_shapes=[pltpu.VMEM((width, width), BF16),
                        pltpu.VMEM((width, PLANE_STATE), BF16),
                        pltpu.VMEM((PLANE_STATE, width), BF16),
                        pltpu.VMEM((rows, width), F32)],
        compiler_params=pltpu.CompilerParams(
            dimension_semantics=("arbitrary", "arbitrary"), vmem_limit_bytes=VMEM_LIMIT),
        name="ssm_scan",
    )(xc, btr, bti, ctr, cti, arow, acol, d_tiled)


def _ssm_tables(lam_re, lam_im, log_dt, b_re, b_im, c_re, c_im, d_skip):
    lam_re = jnp.minimum(lam_re, -1e-4)
    dt = jnp.exp(log_dt)[:, None]
    mag = jnp.exp(lam_re * dt)
    a_re = mag * jnp.cos(lam_im * dt)
    a_im = mag * jnp.sin(lam_im * dt)
    den = lam_re * lam_re + lam_im * lam_im
    p = a_re - 1.0
    f_re = ((p * lam_re + a_im * lam_im) / den)[:, :, None]
    f_im = ((a_im * lam_re - p * lam_im) / den)[:, :, None]
    bb_re = f_re * b_re - f_im * b_im
    bb_im = f_re * b_im + f_im * b_re

    N, C, P, L = SSM_STATE, SSM_GROUP, PLANES, CHUNK

    def bt(bb):
        m = jnp.transpose(bb, (0, 2, 1)).reshape(P, LANES, N)
        return jnp.pad(m, ((0, 0), (0, 0), (0, LANES - N)))

    def ct(c):
        m = jnp.transpose(c, (0, 2, 1)).reshape(P, HALF_STATE, C)
        return jnp.pad(m, ((0, 0), (0, 0), (0, LANES - C)))

    ks = (list(range(L + 1)) + [L << m for m in range(SCAN_STEPS_IN_GROUP)])
    ks += [0] * (CARRY_ROW0 - len(ks)) + [L * (r + 1) for r in range(SUBLANES)]
    ks = jnp.asarray(ks, F32).reshape(-1, 1, 1)
    m = jnp.exp(ks * (lam_re * dt))
    pr = m * jnp.cos(ks * (lam_im * dt))
    pi = m * jnp.sin(ks * (lam_im * dt))
    arow = jnp.concatenate([pr.reshape(APOW_ROWS, P, HALF_STATE),
                            pi.reshape(APOW_ROWS, P, HALF_STATE)], axis=2)
    arow = jnp.transpose(arow, (1, 0, 2))

    def col(pw):
        c = jnp.transpose(pw[:L + 1].reshape(L + 1, P, HALF_STATE), (1, 2, 0))
        return jnp.pad(c, ((0, 0), (0, 0), (0, ACOL_IM - (L + 1))))

    acol = jnp.concatenate([col(pr), col(pi)], axis=2)
    d_tiled = jnp.tile(d_skip.reshape(P, 1, LANES), (1, 1, L))
    return bt(bb_re), bt(bb_im), ct(c_re), ct(c_im), arow, acol, d_tiled


def _mix_out_kernel(g_ref, yb_ref, gate_ref, h_ref, wglu_ref, bglu_ref, wa_ref, wb_ref, wo_ref, o_ref,
                    gs_ref):
    nchunk = g_ref.shape[1]
    for p in range(PLANES):
        for l in range(CHUNK):
            piece = g_ref[p, :, l * LANES:(l + 1) * LANES].astype(F32)
            gs_ref[p, pl.ds(l, nchunk, stride=CHUNK), :] = piece
    g = _load_planes(gs_ref, 0, PLANES)
    glu = _dot(g.astype(BF16), wglu_ref[...]) + bglu_ref[...]
    y_a = (g * jax.nn.sigmoid(glu)).astype(BF16)
    z_a = _dot(y_a, wa_ref[...])
    z_b = _dot(yb_ref[...], wb_ref[...])
    merged = (gate_ref[:, :D_MODEL].astype(F32) * z_a
              + gate_ref[:, D_MODEL:].astype(F32) * z_b).astype(BF16)
    o_ref[...] = h_ref[...] + _dot(merged, wo_ref[...])


def _mix_out(g_planes, y_b, gates, h, w_glu, b_glu, w_a, w_b, w_o, *, tm=512):
    t, d = h.shape

    def const(shape):
        return pl.BlockSpec(shape, lambda i: (0,) * len(shape), pipeline_mode=pl.Buffered(1))

    return pl.pallas_call(
        _mix_out_kernel,
        grid=(t // tm,),
        in_specs=[
            pl.BlockSpec((PLANES, tm // CHUNK, CHUNK * LANES), lambda i: (0, i, 0)),
            pl.BlockSpec((tm, CONV_WIDTH), lambda i: (i, 0)),
            pl.BlockSpec((tm, 2 * d), lambda i: (i, 0)),
            pl.BlockSpec((tm, d), lambda i: (i, 0)),
            const((SSM_WIDTH, SSM_WIDTH)), const((1, SSM_WIDTH)),
            const((SSM_WIDTH, d)), const((CONV_WIDTH, d)), const((d, d)),
        ],
        out_specs=pl.BlockSpec((tm, d), lambda i: (i, 0)),
        out_shape=jax.ShapeDtypeStruct((t, d), F32),
        scratch_shapes=[pltpu.VMEM((PLANES, tm, LANES), F32)],
        compiler_params=pltpu.CompilerParams(
            dimension_semantics=("parallel",), vmem_limit_bytes=VMEM_LIMIT),
        name="mix_out",
    )(g_planes, y_b, gates, h, w_glu, b_glu, w_a, w_b, w_o)


def kernel(x, ffn1_norm, ffn1_w_gate, ffn1_w_up, ffn1_w_down, mix_norm, w_in, ssm_lambda_re, ssm_lambda_im, ssm_log_dt, ssm_b_re, ssm_b_im, ssm_c_re, ssm_c_im, ssm_d, ssm_w_glu, ssm_b_glu, ssm_w_out, conv_w, conv_b, conv_w_out, w_o, ffn2_norm, ffn2_w_gate, ffn2_w_up, ffn2_w_down, final_norm):
    batch, seq, d = x.shape
    t = batch * seq
    chunks = seq // CHUNK

    vec = lambda g: g.reshape(1, -1).astype(F32)

    later = (ffn2_w_gate, ffn2_w_up, ffn2_w_down, w_in, ssm_w_glu, ssm_w_out, conv_w_out, w_o)
    x2 = x.reshape(t, d)
    h1_head, wg1, wu1, wd1 = _ffn_head(x2, vec(ffn1_norm), ffn1_w_gate, ffn1_w_up, ffn1_w_down)
    h1, wg2, wu2, wd2, w_in_b, w_glu_b, w_a_b, w_b_b, w_o_b = _ffn_stream(
        x2, vec(ffn1_norm), wg1, wu1, wd1, head=h1_head, cast=later, tf=512)

    v_chunks, y_b, gates = _in_proj(h1, vec(mix_norm), w_in_b, conv_w.astype(F32), vec(conv_b),
                                    batch=batch, seq=seq)

    tables = _ssm_tables(ssm_lambda_re, ssm_lambda_im, ssm_log_dt, ssm_b_re, ssm_b_im,
                         ssm_c_re, ssm_c_im, ssm_d)
    g_planes = _ssm(v_chunks, *tables, chunks=chunks)

    h2 = _mix_out(g_planes, y_b, gates, h1, w_glu_b, vec(ssm_b_glu), w_a_b, w_b_b, w_o_b)

    out = _ffn_stream(h2, vec(ffn2_norm), wg2, wu2, wd2, vec(final_norm), tf=1024)
    return out.reshape(batch, seq, d)
```

```python
import functools

import jax
import jax.numpy as jnp
from jax import lax
from jax.experimental import pallas as pl
from jax.experimental.pallas import tpu as pltpu

D_MODEL = 2048
SSM_WIDTH = 1024
SSM_GROUP = 16
SSM_STATE = 64
CONV_WIDTH = 1024
CONV_K = 3
EPS = 1e-6

LANES = 128
SUBLANES = 8
BF16_SUBLANES = 16
MXU_DIM = 256
CHUNK = 16
PLANES = SSM_WIDTH // LANES
GROUPS_PER_PLANE = LANES // SSM_GROUP
HALF_STATE = GROUPS_PER_PLANE * SSM_STATE
PLANE_STATE = 2 * HALF_STATE
SCAN_STEPS_IN_GROUP = 3
SCAN_ROW0 = CHUNK + 1
CARRY_ROW0 = -(-(SCAN_ROW0 + SCAN_STEPS_IN_GROUP) // SUBLANES) * SUBLANES
APOW_ROWS = CARRY_ROW0 + SUBLANES
ACOL_IM = LANES // 2
VMEM_BYTES = 64 * 1024 * 1024
VMEM_LIMIT = VMEM_BYTES - 8 * 1024 * 1024
VMEM_LIMIT_FFN = VMEM_BYTES - 4 * 1024 * 1024
VMEM_LIMIT_IN_PROJ = VMEM_BYTES - 2 * 1024 * 1024

BF16 = jnp.bfloat16
F32 = jnp.float32
HIGHEST = lax.Precision.HIGHEST


def _rms(x, g):
    return x * lax.rsqrt(jnp.mean(x * x, axis=-1, keepdims=True) + EPS) * g


def _dot(a, b):
    return jnp.dot(a, b, preferred_element_type=F32)


def _dot_f32(a, b):
    return jnp.dot(a, b, precision=HIGHEST, preferred_element_type=F32)


def _split_bf16(a):
    hi = a.astype(BF16)
    return hi, (a - hi.astype(F32)).astype(BF16)


def _dot_split(a, b_parts):
    a_hi, a_lo = _split_bf16(a)
    b_hi, b_lo = b_parts
    return _dot(a_hi, b_hi) + (_dot(a_lo, b_hi) + _dot(a_hi, b_lo))


def _ffn_prologue(x_ref, gin_ref, xn_ref, o_ref):
    x = x_ref[...]
    xn_ref[...] = _rms(x, gin_ref[...]).astype(BF16)
    o_ref[...] = x


def _ffn_accumulate(xn_ref, o_ref, wg, wu, wd):
    xn = xn_ref[...]
    gate = _dot(xn, wg)
    up = _dot(xn, wu)
    act = ((0.5 * gate) * jax.nn.sigmoid(gate) * up).astype(BF16)
    o_ref[...] += _dot(act, wd)


def _last_hidden_width(d_ff, tf):
    return d_ff - (pl.cdiv(d_ff, tf) - 1) * tf


def _cast_block_rows(nrows, steps):
    br = BF16_SUBLANES
    while nrows % br or nrows // br > steps:
        br += BF16_SUBLANES
    return br


def _ffn_stream_kernel(x_hbm, gin_ref, wg_ref, wu_ref, wd_ref, *refs, final_norm, has_head,
                       d_ff, n_cast):
    if final_norm:
        gout_ref, *refs = refs
    if has_head:
        head_hbm, *refs = refs
    cast_src = refs[:n_cast]
    o_hbm = refs[n_cast]
    cast_dst = refs[n_cast + 1:2 * n_cast + 1]
    xn_ref, acc_ref, in_sem, out_sem = refs[2 * n_cast + 1:]
    i = pl.program_id(0)
    f = pl.program_id(1)
    nt = pl.num_programs(0)
    nf = pl.num_programs(1)
    tm = xn_ref.shape[0]
    tf = wd_ref.shape[0]
    first = 1 if has_head else 0
    live = i >= first
    slot = i % 2
    other = 1 - slot
    acc = acc_ref.at[slot]

    def tile_rows(ref, tile):
        return ref.at[pl.ds(pl.multiple_of(tile * tm, tm), tm)]

    def x_copy(tile, s):
        return pltpu.make_async_copy(tile_rows(x_hbm, tile), acc_ref.at[s], in_sem.at[s])

    def o_copy(tile, s):
        return pltpu.make_async_copy(acc_ref.at[s], tile_rows(o_hbm, tile), out_sem.at[s])

    @pl.when(jnp.logical_and(i == 0, f == 0))
    def _():
        x_copy(first, first % 2).start()
        if has_head:
            head_copy = pltpu.make_async_copy(head_hbm, acc_ref.at[0], in_sem.at[0])
            head_copy.start()
            head_copy.wait()
            o_copy(0, 0).start()

    @pl.when(jnp.logical_and(live, f == 0))
    def _():
        x_copy(i, slot).wait()
        xn_ref[...] = _rms(acc[...], gin_ref[...]).astype(BF16)

    @pl.when(jnp.logical_and(live, f == 1))
    def _():
        @pl.when(i > 0)
        def _():
            o_copy(i - 1, other).wait()

        @pl.when(i + 1 < nt)
        def _():
            x_copy(i + 1, other).start()

    def accumulate(width):
        _ffn_accumulate(xn_ref, acc, wg_ref[:, :width], wu_ref[:, :width], wd_ref[:width, :])

    last_width = _last_hidden_width(d_ff, tf)
    pl.when(jnp.logical_and(live, f < nf - 1))(functools.partial(accumulate, tf))
    pl.when(jnp.logical_and(live, f == nf - 1))(functools.partial(accumulate, last_width))

    @pl.when(jnp.logical_and(live, f == nf - 1))
    def _():
        if final_norm:
            acc[...] = _rms(acc[...], gout_ref[...])
        o_copy(i, slot).start()

        @pl.when(i == nt - 1)
        def _():
            o_copy(i, slot).wait()

    for src, dst in zip(cast_src, cast_dst):
        dst[...] = src[...].astype(dst.dtype)


def _ffn_stream(x, g_in, wg, wu, wd, g_out=None, *, head=None, cast=(), tm=1024, tf):
    t, d = x.shape
    d_ff = wg.shape[1]
    final_norm = g_out is not None
    has_head = head is not None
    nt = t // tm
    nf = pl.cdiv(d_ff, tf)
    assert nf >= 2, "step 1 hands the free accumulator slot to the next tile's input"
    steps = nt * nf
    vec = pl.BlockSpec((1, d), lambda i, f: (0, 0))
    any_space = pl.BlockSpec(memory_space=pl.ANY)
    hidden = (lambda i, f: jnp.where(i == 0, 0, f)) if has_head else (lambda i, f: f)
    in_specs = [any_space, vec,
                pl.BlockSpec((d, tf), lambda i, f: (0, hidden(i, f))),
                pl.BlockSpec((d, tf), lambda i, f: (0, hidden(i, f))),
                pl.BlockSpec((tf, d), lambda i, f: (hidden(i, f), 0))]
    args = [x, g_in, wg, wu, wd]
    if final_norm:
        in_specs.append(vec)
        args.append(g_out)
    if has_head:
        in_specs.append(any_space)
        args.append(head)

    def cast_specs():
        specs = []
        for w in cast:
            nrows, ncols = w.shape
            br = _cast_block_rows(nrows, steps)
            specs.append(pl.BlockSpec(
                (br, ncols), lambda i, f, nb=nrows // br: (jnp.minimum(i * nf + f, nb - 1), 0)))
        return specs

    dma_sems = pltpu.SemaphoreType.DMA
    outs = pl.pallas_call(
        functools.partial(_ffn_stream_kernel, final_norm=final_norm, has_head=has_head,
                          d_ff=d_ff, n_cast=len(cast)),
        grid=(nt, nf),
        in_specs=in_specs + cast_specs(),
        out_specs=[any_space] + cast_specs(),
        out_shape=[jax.ShapeDtypeStruct((t, d), F32)]
        + [jax.ShapeDtypeStruct(w.shape, BF16) for w in cast],
        scratch_shapes=[pltpu.VMEM((tm, d), BF16), pltpu.VMEM((2, tm, d), F32),
                        dma_sems((2,)), dma_sems((2,))],
        compiler_params=pltpu.CompilerParams(
            dimension_semantics=("arbitrary", "arbitrary"),
            vmem_limit_bytes=VMEM_LIMIT_FFN),
        name="ffn_final" if final_norm else "ffn_hidden",
    )(*args, *cast)
    return outs if cast else outs[0]


def _ffn_head_kernel(x_ref, gin_ref, wg_ref, wu_ref, wd_ref, o_ref, wgb_ref, wub_ref, wdb_ref,
                     xn_ref, *, d_ff):
    f = pl.program_id(0)
    nf = pl.num_programs(0)
    tf = wd_ref.shape[0]

    @pl.when(f == 0)
    def _():
        _ffn_prologue(x_ref, gin_ref, xn_ref, o_ref)

    def accumulate(width):
        wgb_ref[:, :width] = wg_ref[:, :width].astype(BF16)
        wub_ref[:, :width] = wu_ref[:, :width].astype(BF16)
        wdb_ref[:width, :] = wd_ref[:width, :].astype(BF16)
        _ffn_accumulate(xn_ref, o_ref, wgb_ref[:, :width], wub_ref[:, :width], wdb_ref[:width, :])

    last_width = _last_hidden_width(d_ff, tf)
    pl.when(f < nf - 1)(functools.partial(accumulate, tf))
    pl.when(f == nf - 1)(functools.partial(accumulate, last_width))


def _ffn_head(x, g_in, wg, wu, wd, *, tm=1024, tf=256):
    d = x.shape[1]
    d_ff = wg.shape[1]
    const = lambda f: (0, 0)
    cols = pl.BlockSpec((d, tf), lambda f: (0, f))
    rows = pl.BlockSpec((tf, d), lambda f: (f, 0))
    return pl.pallas_call(
        functools.partial(_ffn_head_kernel, d_ff=d_ff),
        grid=(pl.cdiv(d_ff, tf),),
        in_specs=[pl.BlockSpec((tm, d), const, pipeline_mode=pl.Buffered(1)),
                  pl.BlockSpec((1, d), const), cols, cols, rows],
        out_specs=[pl.BlockSpec((tm, d), const), cols, cols, rows],
        out_shape=[jax.ShapeDtypeStruct((tm, d), F32),
                   jax.ShapeDtypeStruct(wg.shape, BF16),
                   jax.ShapeDtypeStruct(wu.shape, BF16),
                   jax.ShapeDtypeStruct(wd.shape, BF16)],
        scratch_shapes=[pltpu.VMEM((tm, d), BF16)],
        compiler_params=pltpu.CompilerParams(
            dimension_semantics=("arbitrary",), vmem_limit_bytes=VMEM_LIMIT_FFN),
        name="ffn_head",
    )(x, g_in, wg, wu, wd)


GATE_GROUP0 = 2


def _store_planes(ref, val):
    for p in range(ref.shape[0]):
        ref[p] = val[:, p * LANES:(p + 1) * LANES]


def _load_planes(ref, p0, p1):
    return jnp.concatenate([ref[p] for p in range(p0, p1)], axis=1)


def _in_proj_kernel(h_ref, g_ref, w_ref, cw_ref, cb_ref, v_ref, yb_ref, gate_ref,
                    xn_ref, planes_ref, tail_ref, *, cblk):
    s = pl.program_id(1)
    j = pl.program_id(2)
    ts = h_ref.shape[0]
    nchunk = ts // CHUNK

    @pl.when(j == 0)
    def _ssm_input_and_b_gate():
        xn_ref[...] = _rms(h_ref[...], g_ref[...]).astype(BF16)
        xn = xn_ref[...]
        _store_planes(planes_ref, _dot(xn, w_ref[:, :SSM_WIDTH]))
        for p in range(PLANES):
            for l in range(CHUNK):
                piece = planes_ref[p, pl.ds(l, nchunk, stride=CHUNK), :]
                v_ref[p, :, l * LANES:(l + 1) * LANES] = piece.astype(v_ref.dtype)
        _store_planes(planes_ref, _dot(xn, w_ref[:, SSM_WIDTH:]))

    @pl.when(jnp.logical_and(j == 1, s == 0))
    def _():
        tail_ref[...] = jnp.zeros_like(tail_ref)

    @pl.when(j == 1)
    def _conv():
        xn = xn_ref[...]
        row = lax.broadcasted_iota(jnp.int32, (ts, 1), 0)
        for c in range(CONV_WIDTH // cblk):
            cs = slice(c * cblk, (c + 1) * cblk)
            vs = slice(CONV_WIDTH + c * cblk, CONV_WIDTH + (c + 1) * cblk)
            z = _dot(xn, w_ref[:, cs]) * _dot(xn, w_ref[:, vs])
            prev1 = tail_ref[SUBLANES - 1:SUBLANES, cs]
            prev2 = tail_ref[SUBLANES - 2:SUBLANES - 1, cs]
            z1 = jnp.where(row == 0, prev1, pltpu.roll(z, 1, axis=0))
            z2 = jnp.where(row == 0, prev2, jnp.where(row == 1, prev1, pltpu.roll(z, 2, axis=0)))
            conv = cb_ref[:, cs] + cw_ref[0:1, cs] * z2 + cw_ref[1:2, cs] * z1 + cw_ref[2:3, cs] * z
            b_gate = _load_planes(planes_ref, c * cblk // LANES, (c + 1) * cblk // LANES)
            yb_ref[:, cs] = (b_gate * conv).astype(yb_ref.dtype)
            tail_ref[:, cs] = z[ts - SUBLANES:, :]

    @pl.when(j >= GATE_GROUP0)
    def _gates():
        gate_ref[...] = jax.nn.sigmoid(_dot(xn_ref[...], w_ref[...])).astype(gate_ref.dtype)


def _in_proj(h, g, w, conv_w, conv_b, *, batch, seq, ts=1024, cblk=256):
    t, d = h.shape
    nseq = seq // ts
    gw = 2 * CONV_WIDTH
    ngroups = w.shape[1] // gw
    tok = lambda b, s, j: (b * nseq + s, 0)
    const = lambda b, s, j: (0, 0)
    return pl.pallas_call(
        functools.partial(_in_proj_kernel, cblk=cblk),
        grid=(batch, nseq, ngroups),
        in_specs=[pl.BlockSpec((ts, d), tok),
                  pl.BlockSpec((1, d), const),
                  pl.BlockSpec((d, gw), lambda b, s, j: (0, j)),
                  pl.BlockSpec((CONV_K, CONV_WIDTH), const),
                  pl.BlockSpec((1, CONV_WIDTH), const)],
        out_specs=(pl.BlockSpec((PLANES, ts // CHUNK, CHUNK * LANES),
                                lambda b, s, j: (0, b * nseq + s, 0)),
                   pl.BlockSpec((ts, CONV_WIDTH), tok),
                   pl.BlockSpec((ts, gw),
                                lambda b, s, j: (b * nseq + s, jnp.maximum(j - GATE_GROUP0, 0)))),
        out_shape=(jax.ShapeDtypeStruct((PLANES, t // CHUNK, CHUNK * LANES), BF16),
                   jax.ShapeDtypeStruct((t, CONV_WIDTH), BF16),
                   jax.ShapeDtypeStruct((t, 2 * d), BF16)),
        scratch_shapes=[pltpu.VMEM((ts, d), BF16),
                        pltpu.VMEM((PLANES, ts, LANES), F32),
                        pltpu.VMEM((SUBLANES, CONV_WIDTH), F32)],
        compiler_params=pltpu.CompilerParams(
            dimension_semantics=("arbitrary", "arbitrary", "arbitrary"),
            vmem_limit_bytes=VMEM_LIMIT_IN_PROJ),
        name="in_proj",
    )(h, g, w, conv_w, conv_b)


def _block_diag_tile(src, period, rows_per_group, cols_per_group):
    nrows = src.shape[0]
    ncols = cols_per_group * GROUPS_PER_PLANE
    k = lax.broadcasted_iota(jnp.int32, (src.shape[1], ncols), 0)
    c = lax.broadcasted_iota(jnp.int32, (src.shape[1], ncols), 1)
    tiled = _dot_f32(src, (k == c % period).astype(F32))
    r = lax.broadcasted_iota(jnp.int32, (nrows, ncols), 0)
    c = lax.broadcasted_iota(jnp.int32, (nrows, ncols), 1)
    return jnp.where(r // rows_per_group == c // cols_per_group, tiled, 0.0)


def _ssm_kernel(x_ref, btr_ref, bti_ref, ctr_ref, cti_ref, arow_ref, acol_ref, d_ref, o_ref,
                toep_ref, win_ref, wout_ref, y_ref, *, chunks):
    half = HALF_STATE

    @pl.when(pl.program_id(1) == 0)
    def _build_operators():
        btr = _block_diag_tile(btr_ref[...], SSM_STATE, SSM_GROUP, SSM_STATE)
        bti = _block_diag_tile(bti_ref[...], SSM_STATE, SSM_GROUP, SSM_STATE)
        ctr = _block_diag_tile(ctr_ref[...], SSM_GROUP, SSM_STATE, SSM_GROUP)
        cti = _block_diag_tile(cti_ref[...], SSM_GROUP, SSM_STATE, SSM_GROUP)
        ctr_parts = _split_bf16(ctr)
        cti_parts = _split_bf16(cti)
        zero = jnp.zeros((LANES, LANES), BF16)
        for k in range(CHUNK):
            ar = arow_ref[k:k + 1, :half]
            ai = arow_ref[k:k + 1, half:]
            wr = btr * ar - bti * ai
            wi = btr * ai + bti * ar
            rows = slice((CHUNK - 1 - k) * LANES, (CHUNK - k) * LANES)
            win_ref[rows, :half] = wr.astype(BF16)
            win_ref[rows, half:] = wi.astype(BF16)
            lag = (_dot_split(wr, ctr_parts) - _dot_split(wi, cti_parts)).astype(BF16)
            for lin in range(CHUNK - k):
                lout = lin + k
                toep_ref[lin * LANES:(lin + 1) * LANES, lout * LANES:(lout + 1) * LANES] = lag
        for lin in range(1, CHUNK, MXU_DIM // LANES):
            toep_ref[lin * LANES:(lin + 1) * LANES, (lin - 1) * LANES:lin * LANES] = zero
        for l in range(CHUNK):
            arc = acol_ref[:, l + 1:l + 2]
            aic = acol_ref[:, ACOL_IM + l + 1:ACOL_IM + l + 2]
            cols = slice(l * LANES, (l + 1) * LANES)
            wout_ref[:half, cols] = (ctr * arc - cti * aic).astype(BF16)
            wout_ref[half:, cols] = (-(ctr * aic + cti * arc)).astype(BF16)

    rows, width = x_ref.shape
    s_in = _dot(x_ref[...], win_ref[...])
    sr = s_in[:, :half]
    si = s_in[:, half:]
    for cb in range(width // MXU_DIM):
        cols = slice(cb * MXU_DIM, (cb + 1) * MXU_DIM)
        kk = (cb + 1) * MXU_DIM
        y_ref[:, cols] = (_dot(x_ref[:, :kk], toep_ref[:kk, cols])
                          + d_ref[:, cols] * x_ref[:, cols].astype(F32))
    cidx = lax.broadcasted_iota(jnp.int32, (rows, 1), 0) % chunks
    for k in range(SCAN_STEPS_IN_GROUP):
        sh = 1 << k
        ar = arow_ref[SCAN_ROW0 + k:SCAN_ROW0 + k + 1, :half]
        ai = arow_ref[SCAN_ROW0 + k:SCAN_ROW0 + k + 1, half:]
        keep = cidx % SUBLANES >= sh
        pr = jnp.where(keep, pltpu.roll(sr, sh, axis=0), 0.0)
        pi = jnp.where(keep, pltpu.roll(si, sh, axis=0), 0.0)
        sr, si = sr + (ar * pr - ai * pi), si + (ar * pi + ai * pr)
    cr = arow_ref[CARRY_ROW0:CARRY_ROW0 + SUBLANES, :half]
    ci = arow_ref[CARRY_ROW0:CARRY_ROW0 + SUBLANES, half:]
    out_r, out_i = [], []
    for g in range(rows // SUBLANES):
        gr = sr[g * SUBLANES:(g + 1) * SUBLANES, :]
        gi = si[g * SUBLANES:(g + 1) * SUBLANES, :]
        if g % (chunks // SUBLANES):
            lr = jnp.broadcast_to(out_r[-1][SUBLANES - 1:, :], gr.shape)
            li = jnp.broadcast_to(out_i[-1][SUBLANES - 1:, :], gi.shape)
            gr, gi = gr + (cr * lr - ci * li), gi + (cr * li + ci * lr)
        out_r.append(gr)
        out_i.append(gi)
    sr = jnp.concatenate(out_r, axis=0)
    si = jnp.concatenate(out_i, axis=0)
    first = cidx >= 1
    pr = jnp.where(first, pltpu.roll(sr, 1, axis=0), 0.0).astype(BF16)
    pi = jnp.where(first, pltpu.roll(si, 1, axis=0), 0.0).astype(BF16)
    s_prev = jnp.concatenate([pr, pi], axis=1)
    for cb in range(width // MXU_DIM):
        cols = slice(cb * MXU_DIM, (cb + 1) * MXU_DIM)
        y = y_ref[:, cols] + _dot(s_prev, wout_ref[:, cols])
        o_ref[:, cols] = jax.nn.gelu(y).astype(o_ref.dtype)


def _ssm(xc, btr, bti, ctr, cti, arow, acol, d_tiled, *, chunks, rows=1024):
    planes, nrows, width = xc.shape
    nh = nrows // rows

    def plane(*shape):
        return pl.BlockSpec((None,) + shape, lambda j, h: (j,) + (0,) * len(shape))

    return pl.pallas_call(
        functools.partial(_ssm_kernel, chunks=chunks),
        grid=(planes, nh),
        in_specs=[
            pl.BlockSpec((None, rows, width), lambda j, h: (j, h, 0)),
            plane(LANES, LANES), plane(LANES, LANES),
            plane(HALF_STATE, LANES), plane(HALF_STATE, LANES),
            plane(APOW_ROWS, PLANE_STATE), plane(HALF_STATE, LANES),
            plane(1, width),
        ],
        out_specs=pl.BlockSpec((None, rows, width), lambda j, h: (j, h, 0)),
        out_shape=jax.ShapeDtypeStruct((planes, nrows, width), BF16),
        scratch_shapes=[pltpu.VMEM((width, width), BF16),
                        pltpu.VMEM((width, PLANE_STATE), BF16),
                        pltpu.VMEM((PLANE_STATE, width), BF16),
                        pltpu.VMEM((rows, width), F32)],
        compiler_params=pltpu.CompilerParams(
            dimension_semantics=("arbitrary", "arbitrary"), vmem_limit_bytes=VMEM_LIMIT),
        name="ssm_scan",
    )(xc, btr, bti, ctr, cti, arow, acol, d_tiled)


def _ssm_tables(lam_re, lam_im, log_dt, b_re, b_im, c_re, c_im, d_skip):
    lam_re = jnp.minimum(lam_re, -1e-4)
    dt = jnp.exp(log_dt)[:, None]
    mag = jnp.exp(lam_re * dt)
    a_re = mag * jnp.cos(lam_im * dt)
    a_im = mag * jnp.sin(lam_im * dt)
    den = lam_re * lam_re + lam_im * lam_im
    p = a_re - 1.0
    f_re = ((p * lam_re + a_im * lam_im) / den)[:, :, None]
    f_im = ((a_im * lam_re - p * lam_im) / den)[:, :, None]
    bb_re = f_re * b_re - f_im * b_im
    bb_im = f_re * b_im + f_im * b_re

    N, C, P, L = SSM_STATE, SSM_GROUP, PLANES, CHUNK

    def bt(bb):
        m = jnp.transpose(bb, (0, 2, 1)).reshape(P, LANES, N)
        return jnp.pad(m, ((0, 0), (0, 0), (0, LANES - N)))

    def ct(c):
        m = jnp.transpose(c, (0, 2, 1)).reshape(P, HALF_STATE, C)
        return jnp.pad(m, ((0, 0), (0, 0), (0, LANES - C)))

    ks = (list(range(L + 1)) + [L << m for m in range(SCAN_STEPS_IN_GROUP)])
    ks += [0] * (CARRY_ROW0 - len(ks)) + [L * (r + 1) for r in range(SUBLANES)]
    ks = jnp.asarray(ks, F32).reshape(-1, 1, 1)
    m = jnp.exp(ks * (lam_re * dt))
    pr = m * jnp.cos(ks * (lam_im * dt))
    pi = m * jnp.sin(ks * (lam_im * dt))
    arow = jnp.concatenate([pr.reshape(APOW_ROWS, P, HALF_STATE),
                            pi.reshape(APOW_ROWS, P, HALF_STATE)], axis=2)
    arow = jnp.transpose(arow, (1, 0, 2))

    def col(pw):
        c = jnp.transpose(pw[:L + 1].reshape(L + 1, P, HALF_STATE), (1, 2, 0))
        return jnp.pad(c, ((0, 0), (0, 0), (0, ACOL_IM - (L + 1))))

    acol = jnp.concatenate([col(pr), col(pi)], axis=2)
    d_tiled = jnp.tile(d_skip.reshape(P, 1, LANES), (1, 1, L))
    return bt(bb_re), bt(bb_im), ct(c_re), ct(c_im), arow, acol, d_tiled


def _mix_out_kernel(g_ref, yb_ref, gate_ref, h_ref, wglu_ref, bglu_ref, wa_ref, wb_ref, wo_ref, o_ref,
                    gs_ref):
    nchunk = g_ref.shape[1]
    for p in range(PLANES):
        for l in range(CHUNK):
            piece = g_ref[p, :, l * LANES:(l + 1) * LANES].astype(F32)
            gs_ref[p, pl.ds(l, nchunk, stride=CHUNK), :] = piece
    g = _load_planes(gs_ref, 0, PLANES)
    glu = _dot(g.astype(BF16), wglu_ref[...]) + bglu_ref[...]
    y_a = (g * jax.nn.sigmoid(glu)).astype(BF16)
    z_a = _dot(y_a, wa_ref[...])
    z_b = _dot(yb_ref[...], wb_ref[...])
    merged = (gate_ref[:, :D_MODEL].astype(F32) * z_a
              + gate_ref[:, D_MODEL:].astype(F32) * z_b).astype(BF16)
    o_ref[...] = h_ref[...] + _dot(merged, wo_ref[...])


def _mix_out(g_planes, y_b, gates, h, w_glu, b_glu, w_a, w_b, w_o, *, tm=512):
    t, d = h.shape

    def const(shape):
        return pl.BlockSpec(shape, lambda i: (0,) * len(shape), pipeline_mode=pl.Buffered(1))

    return pl.pallas_call(
        _mix_out_kernel,
        grid=(t // tm,),
        in_specs=[
            pl.BlockSpec((PLANES, tm // CHUNK, CHUNK * LANES), lambda i: (0, i, 0)),
            pl.BlockSpec((tm, CONV_WIDTH), lambda i: (i, 0)),
            pl.BlockSpec((tm, 2 * d), lambda i: (i, 0)),
            pl.BlockSpec((tm, d), lambda i: (i, 0)),
            const((SSM_WIDTH, SSM_WIDTH)), const((1, SSM_WIDTH)),
            const((SSM_WIDTH, d)), const((CONV_WIDTH, d)), const((d, d)),
        ],
        out_specs=pl.BlockSpec((tm, d), lambda i: (i, 0)),
        out_shape=jax.ShapeDtypeStruct((t, d), F32),
        scratch_shapes=[pltpu.VMEM((PLANES, tm, LANES), F32)],
        compiler_params=pltpu.CompilerParams(
            dimension_semantics=("parallel",), vmem_limit_bytes=VMEM_LIMIT),
        name="mix_out",
    )(g_planes, y_b, gates, h, w_glu, b_glu, w_a, w_b, w_o)


def kernel(x, ffn1_norm, ffn1_w_gate, ffn1_w_up, ffn1_w_down, mix_norm, w_in, ssm_lambda_re, ssm_lambda_im, ssm_log_dt, ssm_b_re, ssm_b_im, ssm_c_re, ssm_c_im, ssm_d, ssm_w_glu, ssm_b_glu, ssm_w_out, conv_w, conv_b, conv_w_out, w_o, ffn2_norm, ffn2_w_gate, ffn2_w_up, ffn2_w_down, final_norm):
    batch, seq, d = x.shape
    t = batch * seq
    chunks = seq // CHUNK

    vec = lambda g: g.reshape(1, -1).astype(F32)

    later = (ffn2_w_gate, ffn2_w_up, ffn2_w_down, w_in, ssm_w_glu, ssm_w_out, conv_w_out, w_o)
    x2 = x.reshape(t, d)
    h1_head, wg1, wu1, wd1 = _ffn_head(x2, vec(ffn1_norm), ffn1_w_gate, ffn1_w_up, ffn1_w_down)
    h1, wg2, wu2, wd2, w_in_b, w_glu_b, w_a_b, w_b_b, w_o_b = _ffn_stream(
        x2, vec(ffn1_norm), wg1, wu1, wd1, head=h1_head, cast=later, tf=512)

    v_chunks, y_b, gates = _in_proj(h1, vec(mix_norm), w_in_b, conv_w.astype(F32), vec(conv_b),
                                    batch=batch, seq=seq)

    tables = _ssm_tables(ssm_lambda_re, ssm_lambda_im, ssm_log_dt, ssm_b_re, ssm_b_im,
                         ssm_c_re, ssm_c_im, ssm_d)
    g_planes = _ssm(v_chunks, *tables, chunks=chunks)

    h2 = _mix_out(g_planes, y_b, gates, h1, w_glu_b, vec(ssm_b_glu), w_a_b, w_b_b, w_o_b)

    out = _ffn_stream(h2, vec(ffn2_norm), wg2, wu2, wd2, vec(final_norm), tf=1024)
    return out.reshape(batch, seq, d)
```

```python
import functools

import jax
import jax.numpy as jnp
from jax import lax
from jax.experimental import pallas as pl
from jax.experimental.pallas import tpu as pltpu

D_MODEL = 2048
SSM_WIDTH = 1024
SSM_GROUP = 16
SSM_STATE = 64
CONV_WIDTH = 1024
CONV_K = 3
EPS = 1e-6

LANES = 128
SUBLANES = 8
BF16_SUBLANES = 16
MXU_DIM = 256
CHUNK = 16
PLANES = SSM_WIDTH // LANES
GROUPS_PER_PLANE = LANES // SSM_GROUP
HALF_STATE = GROUPS_PER_PLANE * SSM_STATE
PLANE_STATE = 2 * HALF_STATE
SCAN_STEPS_IN_GROUP = 3
SCAN_ROW0 = CHUNK + 1
CARRY_ROW0 = -(-(SCAN_ROW0 + SCAN_STEPS_IN_GROUP) // SUBLANES) * SUBLANES
APOW_ROWS = CARRY_ROW0 + SUBLANES
ACOL_IM = LANES // 2
VMEM_BYTES = 64 * 1024 * 1024
VMEM_LIMIT = VMEM_BYTES - 8 * 1024 * 1024
VMEM_LIMIT_FFN = VMEM_BYTES - 4 * 1024 * 1024
VMEM_LIMIT_IN_PROJ = VMEM_BYTES - 2 * 1024 * 1024

BF16 = jnp.bfloat16
F32 = jnp.float32
HIGHEST = lax.Precision.HIGHEST


def _rms(x, g):
    return x * lax.rsqrt(jnp.mean(x * x, axis=-1, keepdims=True) + EPS) * g


def _dot(a, b):
    return jnp.dot(a, b, preferred_element_type=F32)


def _dot_f32(a, b):
    return jnp.dot(a, b, precision=HIGHEST, preferred_element_type=F32)


def _ffn_prologue(x_ref, gin_ref, xn_ref, o_ref):
    x = x_ref[...]
    xn_ref[...] = _rms(x, gin_ref[...]).astype(BF16)
    o_ref[...] = x


def _ffn_accumulate(xn_ref, o_ref, wg, wu, wd):
    xn = xn_ref[...]
    gate = _dot(xn, wg)
    up = _dot(xn, wu)
    act = ((0.5 * gate) * jax.nn.sigmoid(gate) * up).astype(BF16)
    o_ref[...] += _dot(act, wd)


def _last_hidden_width(d_ff, tf):
    return d_ff - (pl.cdiv(d_ff, tf) - 1) * tf


def _cast_block_rows(nrows, steps):
    br = BF16_SUBLANES
    while nrows % br or nrows // br > steps:
        br += BF16_SUBLANES
    return br


def _ffn_stream_kernel(x_hbm, gin_ref, wg_ref, wu_ref, wd_ref, *refs, final_norm, has_head,
                       d_ff, n_cast):
    if final_norm:
        gout_ref, *refs = refs
    if has_head:
        head_hbm, *refs = refs
    cast_src = refs[:n_cast]
    o_hbm = refs[n_cast]
    cast_dst = refs[n_cast + 1:2 * n_cast + 1]
    xn_ref, acc_ref, in_sem, out_sem = refs[2 * n_cast + 1:]
    i = pl.program_id(0)
    f = pl.program_id(1)
    nt = pl.num_programs(0)
    nf = pl.num_programs(1)
    tm = xn_ref.shape[0]
    tf = wd_ref.shape[0]
    first = 1 if has_head else 0
    live = i >= first
    slot = i % 2
    other = 1 - slot
    acc = acc_ref.at[slot]

    def tile_rows(ref, tile):
        return ref.at[pl.ds(pl.multiple_of(tile * tm, tm), tm)]

    def x_copy(tile, s):
        return pltpu.make_async_copy(tile_rows(x_hbm, tile), acc_ref.at[s], in_sem.at[s])

    def o_copy(tile, s):
        return pltpu.make_async_copy(acc_ref.at[s], tile_rows(o_hbm, tile), out_sem.at[s])

    @pl.when(jnp.logical_and(i == 0, f == 0))
    def _():
        x_copy(first, first % 2).start()
        if has_head:
            head_copy = pltpu.make_async_copy(head_hbm, acc_ref.at[0], in_sem.at[0])
            head_copy.start()
            head_copy.wait()
            o_copy(0, 0).start()

    @pl.when(jnp.logical_and(live, f == 0))
    def _():
        x_copy(i, slot).wait()
        xn_ref[...] = _rms(acc[...], gin_ref[...]).astype(BF16)

    @pl.when(jnp.logical_and(live, f == 1))
    def _():
        @pl.when(i > 0)
        def _():
            o_copy(i - 1, other).wait()

        @pl.when(i + 1 < nt)
        def _():
            x_copy(i + 1, other).start()

    def accumulate(width):
        _ffn_accumulate(xn_ref, acc, wg_ref[:, :width], wu_ref[:, :width], wd_ref[:width, :])

    last_width = _last_hidden_width(d_ff, tf)
    pl.when(jnp.logical_and(live, f < nf - 1))(functools.partial(accumulate, tf))
    pl.when(jnp.logical_and(live, f == nf - 1))(functools.partial(accumulate, last_width))

    @pl.when(jnp.logical_and(live, f == nf - 1))
    def _():
        if final_norm:
            acc[...] = _rms(acc[...], gout_ref[...])
        o_copy(i, slot).start()

        @pl.when(i == nt - 1)
        def _():
            o_copy(i, slot).wait()

    for src, dst in zip(cast_src, cast_dst):
        dst[...] = src[...].astype(dst.dtype)


def _ffn_stream(x, g_in, wg, wu, wd, g_out=None, *, head=None, cast=(), tm=1024, tf):
    t, d = x.shape
    d_ff = wg.shape[1]
    final_norm = g_out is not None
    has_head = head is not None
    nt = t // tm
    nf = pl.cdiv(d_ff, tf)
    assert nf >= 2, "step 1 hands the free accumulator slot to the next tile's input"
    steps = nt * nf
    vec = pl.BlockSpec((1, d), lambda i, f: (0, 0))
    any_space = pl.BlockSpec(memory_space=pl.ANY)
    hidden = (lambda i, f: jnp.where(i == 0, 0, f)) if has_head else (lambda i, f: f)
    in_specs = [any_space, vec,
                pl.BlockSpec((d, tf), lambda i, f: (0, hidden(i, f))),
                pl.BlockSpec((d, tf), lambda i, f: (0, hidden(i, f))),
                pl.BlockSpec((tf, d), lambda i, f: (hidden(i, f), 0))]
    args = [x, g_in, wg, wu, wd]
    if final_norm:
        in_specs.append(vec)
        args.append(g_out)
    if has_head:
        in_specs.append(any_space)
        args.append(head)

    def cast_specs():
        specs = []
        for w in cast:
            nrows, ncols = w.shape
            br = _cast_block_rows(nrows, steps)
            specs.append(pl.BlockSpec(
                (br, ncols), lambda i, f, nb=nrows // br: (jnp.minimum(i * nf + f, nb - 1), 0)))
        return specs

    dma_sems = pltpu.SemaphoreType.DMA
    outs = pl.pallas_call(
        functools.partial(_ffn_stream_kernel, final_norm=final_norm, has_head=has_head,
                          d_ff=d_ff, n_cast=len(cast)),
        grid=(nt, nf),
        in_specs=in_specs + cast_specs(),
        out_specs=[any_space] + cast_specs(),
        out_shape=[jax.ShapeDtypeStruct((t, d), F32)]
        + [jax.ShapeDtypeStruct(w.shape, BF16) for w in cast],
        scratch_shapes=[pltpu.VMEM((tm, d), BF16), pltpu.VMEM((2, tm, d), F32),
                        dma_sems((2,)), dma_sems((2,))],
        compiler_params=pltpu.CompilerParams(
            dimension_semantics=("arbitrary", "arbitrary"),
            vmem_limit_bytes=VMEM_LIMIT_FFN),
        name="ffn_final" if final_norm else "ffn_hidden",
    )(*args, *cast)
    return outs if cast else outs[0]


def _ffn_head_kernel(x_ref, gin_ref, wg_ref, wu_ref, wd_ref, o_ref, wgb_ref, wub_ref, wdb_ref,
                     xn_ref, *, d_ff):
    f = pl.program_id(0)
    nf = pl.num_programs(0)
    tf = wd_ref.shape[0]

    @pl.when(f == 0)
    def _():
        _ffn_prologue(x_ref, gin_ref, xn_ref, o_ref)

    def accumulate(width):
        wgb_ref[:, :width] = wg_ref[:, :width].astype(BF16)
        wub_ref[:, :width] = wu_ref[:, :width].astype(BF16)
        wdb_ref[:width, :] = wd_ref[:width, :].astype(BF16)
        _ffn_accumulate(xn_ref, o_ref, wgb_ref[:, :width], wub_ref[:, :width], wdb_ref[:width, :])

    last_width = _last_hidden_width(d_ff, tf)
    pl.when(f < nf - 1)(functools.partial(accumulate, tf))
    pl.when(f == nf - 1)(functools.partial(accumulate, last_width))


def _ffn_head(x, g_in, wg, wu, wd, *, tm=1024, tf=256):
    d = x.shape[1]
    d_ff = wg.shape[1]
    const = lambda f: (0, 0)
    cols = pl.BlockSpec((d, tf), lambda f: (0, f))
    rows = pl.BlockSpec((tf, d), lambda f: (f, 0))
    return pl.pallas_call(
        functools.partial(_ffn_head_kernel, d_ff=d_ff),
        grid=(pl.cdiv(d_ff, tf),),
        in_specs=[pl.BlockSpec((tm, d), const, pipeline_mode=pl.Buffered(1)),
                  pl.BlockSpec((1, d), const), cols, cols, rows],
        out_specs=[pl.BlockSpec((tm, d), const), cols, cols, rows],
        out_shape=[jax.ShapeDtypeStruct((tm, d), F32),
                   jax.ShapeDtypeStruct(wg.shape, BF16),
                   jax.ShapeDtypeStruct(wu.shape, BF16),
                   jax.ShapeDtypeStruct(wd.shape, BF16)],
        scratch_shapes=[pltpu.VMEM((tm, d), BF16)],
        compiler_params=pltpu.CompilerParams(
            dimension_semantics=("arbitrary",), vmem_limit_bytes=VMEM_LIMIT_FFN),
        name="ffn_head",
    )(x, g_in, wg, wu, wd)


GATE_GROUP0 = 2


def _store_planes(ref, val):
    for p in range(ref.shape[0]):
        ref[p] = val[:, p * LANES:(p + 1) * LANES]


def _load_planes(ref, p0, p1):
    return jnp.concatenate([ref[p] for p in range(p0, p1)], axis=1)


def _in_proj_kernel(h_ref, g_ref, w_ref, cw_ref, cb_ref, v_ref, yb_ref, gate_ref,
                    xn_ref, planes_ref, tail_ref, *, cblk):
    s = pl.program_id(1)
    j = pl.program_id(2)
    ts = h_ref.shape[0]
    nchunk = ts // CHUNK

    @pl.when(j == 0)
    def _ssm_input_and_b_gate():
        xn_ref[...] = _rms(h_ref[...], g_ref[...]).astype(BF16)
        xn = xn_ref[...]
        _store_planes(planes_ref, _dot(xn, w_ref[:, :SSM_WIDTH]))
        for p in range(PLANES):
            for l in range(CHUNK):
                piece = planes_ref[p, pl.ds(l, nchunk, stride=CHUNK), :]
                v_ref[p, :, l * LANES:(l + 1) * LANES] = piece.astype(v_ref.dtype)
        _store_planes(planes_ref, _dot(xn, w_ref[:, SSM_WIDTH:]))

    @pl.when(jnp.logical_and(j == 1, s == 0))
    def _():
        tail_ref[...] = jnp.zeros_like(tail_ref)

    @pl.when(j == 1)
    def _conv():
        xn = xn_ref[...]
        row = lax.broadcasted_iota(jnp.int32, (ts, 1), 0)
        for c in range(CONV_WIDTH // cblk):
            cs = slice(c * cblk, (c + 1) * cblk)
            vs = slice(CONV_WIDTH + c * cblk, CONV_WIDTH + (c + 1) * cblk)
            z = _dot(xn, w_ref[:, cs]) * _dot(xn, w_ref[:, vs])
            prev1 = tail_ref[SUBLANES - 1:SUBLANES, cs]
            prev2 = tail_ref[SUBLANES - 2:SUBLANES - 1, cs]
            z1 = jnp.where(row == 0, prev1, pltpu.roll(z, 1, axis=0))
            z2 = jnp.where(row == 0, prev2, jnp.where(row == 1, prev1, pltpu.roll(z, 2, axis=0)))
            conv = cb_ref[:, cs] + cw_ref[0:1, cs] * z2 + cw_ref[1:2, cs] * z1 + cw_ref[2:3, cs] * z
            b_gate = _load_planes(planes_ref, c * cblk // LANES, (c + 1) * cblk // LANES)
            yb_ref[:, cs] = (b_gate * conv).astype(yb_ref.dtype)
            tail_ref[:, cs] = z[ts - SUBLANES:, :]

    @pl.when(j >= GATE_GROUP0)
    def _gates():
        gate_ref[...] = jax.nn.sigmoid(_dot(xn_ref[...], w_ref[...])).astype(gate_ref.dtype)


def _in_proj(h, g, w, conv_w, conv_b, *, batch, seq, ts=1024, cblk=256):
    t, d = h.shape
    nseq = seq // ts
    gw = 2 * CONV_WIDTH
    ngroups = w.shape[1] // gw
    tok = lambda b, s, j: (b * nseq + s, 0)
    const = lambda b, s, j: (0, 0)
    return pl.pallas_call(
        functools.partial(_in_proj_kernel, cblk=cblk),
        grid=(batch, nseq, ngroups),
        in_specs=[pl.BlockSpec((ts, d), tok),
                  pl.BlockSpec((1, d), const),
                  pl.BlockSpec((d, gw), lambda b, s, j: (0, j)),
                  pl.BlockSpec((CONV_K, CONV_WIDTH), const),
                  pl.BlockSpec((1, CONV_WIDTH), const)],
        out_specs=(pl.BlockSpec((PLANES, ts // CHUNK, CHUNK * LANES),
                                lambda b, s, j: (0, b * nseq + s, 0)),
                   pl.BlockSpec((ts, CONV_WIDTH), tok),
                   pl.BlockSpec((ts, gw),
                                lambda b, s, j: (b * nseq + s, jnp.maximum(j - GATE_GROUP0, 0)))),
        out_shape=(jax.ShapeDtypeStruct((PLANES, t // CHUNK, CHUNK * LANES), BF16),
                   jax.ShapeDtypeStruct((t, CONV_WIDTH), BF16),
                   jax.ShapeDtypeStruct((t, 2 * d), BF16)),
        scratch_shapes=[pltpu.VMEM((ts, d), BF16),
                        pltpu.VMEM((PLANES, ts, LANES), F32),
                        pltpu.VMEM((SUBLANES, CONV_WIDTH), F32)],
        compiler_params=pltpu.CompilerParams(
            dimension_semantics=("arbitrary", "arbitrary", "arbitrary"),
            vmem_limit_bytes=VMEM_LIMIT_IN_PROJ),
        name="in_proj",
    )(h, g, w, conv_w, conv_b)


def _block_diag_tile(src, period, rows_per_group, cols_per_group):
    nrows = src.shape[0]
    ncols = cols_per_group * GROUPS_PER_PLANE
    k = lax.broadcasted_iota(jnp.int32, (src.shape[1], ncols), 0)
    c = lax.broadcasted_iota(jnp.int32, (src.shape[1], ncols), 1)
    tiled = _dot_f32(src, (k == c % period).astype(F32))
    r = lax.broadcasted_iota(jnp.int32, (nrows, ncols), 0)
    c = lax.broadcasted_iota(jnp.int32, (nrows, ncols), 1)
    return jnp.where(r // rows_per_group == c // cols_per_group, tiled, 0.0)


def _ssm_kernel(x_ref, btr_ref, bti_ref, ctr_ref, cti_ref, arow_ref, acol_ref, d_ref, o_ref,
                toep_ref, win_ref, wout_ref, y_ref, *, chunks):
    half = HALF_STATE

    @pl.when(pl.program_id(1) == 0)
    def _build_operators():
        btr = _block_diag_tile(btr_ref[...], SSM_STATE, SSM_GROUP, SSM_STATE)
        bti = _block_diag_tile(bti_ref[...], SSM_STATE, SSM_GROUP, SSM_STATE)
        ctr = _block_diag_tile(ctr_ref[...], SSM_GROUP, SSM_STATE, SSM_GROUP)
        cti = _block_diag_tile(cti_ref[...], SSM_GROUP, SSM_STATE, SSM_GROUP)
        ctr_b = ctr.astype(BF16)
        cti_b = cti.astype(BF16)
        zero = jnp.zeros((LANES, LANES), BF16)
        for k in range(CHUNK):
            ar = arow_ref[k:k + 1, :half]
            ai = arow_ref[k:k + 1, half:]
            wr = btr * ar - bti * ai
            wi = btr * ai + bti * ar
            rows = slice((CHUNK - 1 - k) * LANES, (CHUNK - k) * LANES)
            wr_b = wr.astype(BF16)
            wi_b = wi.astype(BF16)
            win_ref[rows, :half] = wr_b
            win_ref[rows, half:] = wi_b
            lag = (_dot(wr_b, ctr_b) - _dot(wi_b, cti_b)).astype(BF16)
            for lin in range(CHUNK - k):
                lout = lin + k
                toep_ref[lin * LANES:(lin + 1) * LANES, lout * LANES:(lout + 1) * LANES] = lag
        for lin in range(1, CHUNK, MXU_DIM // LANES):
            toep_ref[lin * LANES:(lin + 1) * LANES, (lin - 1) * LANES:lin * LANES] = zero
        for l in range(CHUNK):
            arc = acol_ref[:, l + 1:l + 2]
            aic = acol_ref[:, ACOL_IM + l + 1:ACOL_IM + l + 2]
            cols = slice(l * LANES, (l + 1) * LANES)
            wout_ref[:half, cols] = (ctr * arc - cti * aic).astype(BF16)
            wout_ref[half:, cols] = (-(ctr * aic + cti * arc)).astype(BF16)

    rows, width = x_ref.shape
    s_in = _dot(x_ref[...], win_ref[...])
    sr = s_in[:, :half]
    si = s_in[:, half:]
    for cb in range(width // MXU_DIM):
        cols = slice(cb * MXU_DIM, (cb + 1) * MXU_DIM)
        kk = (cb + 1) * MXU_DIM
        y_ref[:, cols] = (_dot(x_ref[:, :kk], toep_ref[:kk, cols])
                          + d_ref[:, cols] * x_ref[:, cols].astype(F32))
    cidx = lax.broadcasted_iota(jnp.int32, (rows, 1), 0) % chunks
    for k in range(SCAN_STEPS_IN_GROUP):
        sh = 1 << k
        ar = arow_ref[SCAN_ROW0 + k:SCAN_ROW0 + k + 1, :half]
        ai = arow_ref[SCAN_ROW0 + k:SCAN_ROW0 + k + 1, half:]
        keep = cidx % SUBLANES >= sh
        pr = jnp.where(keep, pltpu.roll(sr, sh, axis=0), 0.0)
        pi = jnp.where(keep, pltpu.roll(si, sh, axis=0), 0.0)
        sr, si = sr + (ar * pr - ai * pi), si + (ar * pi + ai * pr)
    cr = arow_ref[CARRY_ROW0:CARRY_ROW0 + SUBLANES, :half]
    ci = arow_ref[CARRY_ROW0:CARRY_ROW0 + SUBLANES, half:]
    out_r, out_i = [], []
    for g in range(rows // SUBLANES):
        gr = sr[g * SUBLANES:(g + 1) * SUBLANES, :]
        gi = si[g * SUBLANES:(g + 1) * SUBLANES, :]
        if g % (chunks // SUBLANES):
            lr = jnp.broadcast_to(out_r[-1][SUBLANES - 1:, :], gr.shape)
            li = jnp.broadcast_to(out_i[-1][SUBLANES - 1:, :], gi.shape)
            gr, gi = gr + (cr * lr - ci * li), gi + (cr * li + ci * lr)
        out_r.append(gr)
        out_i.append(gi)
    sr = jnp.concatenate(out_r, axis=0)
    si = jnp.concatenate(out_i, axis=0)
    first = cidx >= 1
    pr = jnp.where(first, pltpu.roll(sr, 1, axis=0), 0.0).astype(BF16)
    pi = jnp.where(first, pltpu.roll(si, 1, axis=0), 0.0).astype(BF16)
    s_prev = jnp.concatenate([pr, pi], axis=1)
    for cb in range(width // MXU_DIM):
        cols = slice(cb * MXU_DIM, (cb + 1) * MXU_DIM)
        y = y_ref[:, cols] + _dot(s_prev, wout_ref[:, cols])
        o_ref[:, cols] = jax.nn.gelu(y).astype(o_ref.dtype)


def _ssm(xc, btr, bti, ctr, cti, arow, acol, d_tiled, *, chunks, rows=1024):
    planes, nrows, width = xc.shape
    nh = nrows // rows

    def plane(*shape):
        return pl.BlockSpec((None,) + shape, lambda j, h: (j,) + (0,) * len(shape))

    return pl.pallas_call(
        functools.partial(_ssm_kernel, chunks=chunks),
        grid=(planes, nh),
        in_specs=[
            pl.BlockSpec((None, rows, width), lambda j, h: (j, h, 0)),
            plane(LANES, LANES), plane(LANES, LANES),
            plane(HALF_STATE, LANES), plane(HALF_STATE, LANES),
            plane(APOW_ROWS, PLANE_STATE), plane(HALF_STATE, LANES),
            plane(1, width),
        ],
        out_specs=pl.BlockSpec((None, rows, width), lambda j, h: (j, h, 0)),
        out_shape=jax.ShapeDtypeStruct((planes, nrows, width), BF16),
        scratch_shapes=[pltpu.VMEM((width, width), BF16),
                        pltpu.VMEM((width, PLANE_STATE), BF16),
                        pltpu.VMEM((PLANE_STATE, width), BF16),
                        pltpu.VMEM((rows, width), F32)],
        compiler_params=pltpu.CompilerParams(
            dimension_semantics=("arbitrary", "arbitrary"), vmem_limit_bytes=VMEM_LIMIT),
        name="ssm_scan",
    )(xc, btr, bti, ctr, cti, arow, acol, d_tiled)


def _ssm_tables(lam_re, lam_im, log_dt, b_re, b_im, c_re, c_im, d_skip):
    lam_re = jnp.minimum(lam_re, -1e-4)
    dt = jnp.exp(log_dt)[:, None]
    mag = jnp.exp(lam_re * dt)
    a_re = mag * jnp.cos(lam_im * dt)
    a_im = mag * jnp.sin(lam_im * dt)
    den = lam_re * lam_re + lam_im * lam_im
    p = a_re - 1.0
    f_re = ((p * lam_re + a_im * lam_im) / den)[:, :, None]
    f_im = ((a_im * lam_re - p * lam_im) / den)[:, :, None]
    bb_re = f_re * b_re - f_im * b_im
    bb_im = f_re * b_im + f_im * b_re

    N, C, P, L = SSM_STATE, SSM_GROUP, PLANES, CHUNK

    def bt(bb):
        m = jnp.transpose(bb, (0, 2, 1)).reshape(P, LANES, N)
        return jnp.pad(m, ((0, 0), (0, 0), (0, LANES - N)))

    def ct(c):
        m = jnp.transpose(c, (0, 2, 1)).reshape(P, HALF_STATE, C)
        return jnp.pad(m, ((0, 0), (0, 0), (0, LANES - C)))

    ks = (list(range(L + 1)) + [L << m for m in range(SCAN_STEPS_IN_GROUP)])
    ks += [0] * (CARRY_ROW0 - len(ks)) + [L * (r + 1) for r in range(SUBLANES)]
    ks = jnp.asarray(ks, F32).reshape(-1, 1, 1)
    m = jnp.exp(ks * (lam_re * dt))
    pr = m * jnp.cos(ks * (lam_im * dt))
    pi = m * jnp.sin(ks * (lam_im * dt))
    arow = jnp.concatenate([pr.reshape(APOW_ROWS, P, HALF_STATE),
                            pi.reshape(APOW_ROWS, P, HALF_STATE)], axis=2)
    arow = jnp.transpose(arow, (1, 0, 2))

    def col(pw):
        c = jnp.transpose(pw[:L + 1].reshape(L + 1, P, HALF_STATE), (1, 2, 0))
        return jnp.pad(c, ((0, 0), (0, 0), (0, ACOL_IM - (L + 1))))

    acol = jnp.concatenate([col(pr), col(pi)], axis=2)
    d_tiled = jnp.tile(d_skip.reshape(P, 1, LANES), (1, 1, L))
    return bt(bb_re), bt(bb_im), ct(c_re), ct(c_im), arow, acol, d_tiled


def _mix_out_kernel(g_ref, yb_ref, gate_ref, h_ref, wglu_ref, bglu_ref, wa_ref, wb_ref, wo_ref, o_ref,
                    gs_ref):
    nchunk = g_ref.shape[1]
    for p in range(PLANES):
        for l in range(CHUNK):
            piece = g_ref[p, :, l * LANES:(l + 1) * LANES].astype(F32)
            gs_ref[p, pl.ds(l, nchunk, stride=CHUNK), :] = piece
    g = _load_planes(gs_ref, 0, PLANES)
    glu = _dot(g.astype(BF16), wglu_ref[...]) + bglu_ref[...]
    y_a = (g * jax.nn.sigmoid(glu)).astype(BF16)
    z_a = _dot(y_a, wa_ref[...])
    z_b = _dot(yb_ref[...], wb_ref[...])
    merged = (gate_ref[:, :D_MODEL].astype(F32) * z_a
              + gate_ref[:, D_MODEL:].astype(F32) * z_b).astype(BF16)
    o_ref[...] = h_ref[...] + _dot(merged, wo_ref[...])


def _mix_out(g_planes, y_b, gates, h, w_glu, b_glu, w_a, w_b, w_o, *, tm=512):
    t, d = h.shape

    def const(shape):
        return pl.BlockSpec(shape, lambda i: (0,) * len(shape), pipeline_mode=pl.Buffered(1))

    return pl.pallas_call(
        _mix_out_kernel,
        grid=(t // tm,),
        in_specs=[
            pl.BlockSpec((PLANES, tm // CHUNK, CHUNK * LANES), lambda i: (0, i, 0)),
            pl.BlockSpec((tm, CONV_WIDTH), lambda i: (i, 0)),
            pl.BlockSpec((tm, 2 * d), lambda i: (i, 0)),
            pl.BlockSpec((tm, d), lambda i: (i, 0)),
            const((SSM_WIDTH, SSM_WIDTH)), const((1, SSM_WIDTH)),
            const((SSM_WIDTH, d)), const((CONV_WIDTH, d)), const((d, d)),
        ],
        out_specs=pl.BlockSpec((tm, d), lambda i: (i, 0)),
        out_shape=jax.ShapeDtypeStruct((t, d), F32),
        scratch_shapes=[pltpu.VMEM((PLANES, tm, LANES), F32)],
        compiler_params=pltpu.CompilerParams(
            dimension_semantics=("parallel",), vmem_limit_bytes=VMEM_LIMIT),
        name="mix_out",
    )(g_planes, y_b, gates, h, w_glu, b_glu, w_a, w_b, w_o)


def kernel(x, ffn1_norm, ffn1_w_gate, ffn1_w_up, ffn1_w_down, mix_norm, w_in, ssm_lambda_re, ssm_lambda_im, ssm_log_dt, ssm_b_re, ssm_b_im, ssm_c_re, ssm_c_im, ssm_d, ssm_w_glu, ssm_b_glu, ssm_w_out, conv_w, conv_b, conv_w_out, w_o, ffn2_norm, ffn2_w_gate, ffn2_w_up, ffn2_w_down, final_norm):
    batch, seq, d = x.shape
    t = batch * seq
    chunks = seq // CHUNK

    vec = lambda g: g.reshape(1, -1).astype(F32)

    later = (ffn2_w_gate, ffn2_w_up, ffn2_w_down, w_in, ssm_w_glu, ssm_w_out, conv_w_out, w_o)
    x2 = x.reshape(t, d)
    h1_head, wg1, wu1, wd1 = _ffn_head(x2, vec(ffn1_norm), ffn1_w_gate, ffn1_w_up, ffn1_w_down)
    h1, wg2, wu2, wd2, w_in_b, w_glu_b, w_a_b, w_b_b, w_o_b = _ffn_stream(
        x2, vec(ffn1_norm), wg1, wu1, wd1, head=h1_head, cast=later, tf=512)

    v_chunks, y_b, gates = _in_proj(h1, vec(mix_norm), w_in_b, conv_w.astype(F32), vec(conv_b),
                                    batch=batch, seq=seq)

    tables = _ssm_tables(ssm_lambda_re, ssm_lambda_im, ssm_log_dt, ssm_b_re, ssm_b_im,
                         ssm_c_re, ssm_c_im, ssm_d)
    g_planes = _ssm(v_chunks, *tables, chunks=chunks)

    h2 = _mix_out(g_planes, y_b, gates, h1, w_glu_b, vec(ssm_b_glu), w_a_b, w_b_b, w_o_b)

    out = _ffn_stream(h2, vec(ffn2_norm), wg2, wu2, wd2, vec(final_norm), tf=1024)
    return out.reshape(batch, seq, d)
```

```python
import functools

import jax
import jax.numpy as jnp
from jax import lax
from jax.experimental import pallas as pl
from jax.experimental.pallas import tpu as pltpu

D_MODEL = 2048
SSM_WIDTH = 1024
SSM_GROUP = 16
SSM_STATE = 64
CONV_WIDTH = 1024
CONV_K = 3
EPS = 1e-6

LANES = 128
SUBLANES = 8
BF16_SUBLANES = 16
MXU_DIM = 256
CHUNK = 16
PLANES = SSM_WIDTH // LANES
GROUPS_PER_PLANE = LANES // SSM_GROUP
HALF_STATE = GROUPS_PER_PLANE * SSM_STATE
PLANE_STATE = 2 * HALF_STATE
SCAN_STEPS_IN_GROUP = 3
SCAN_ROW0 = CHUNK + 1
CARRY_ROW0 = -(-(SCAN_ROW0 + SCAN_STEPS_IN_GROUP) // SUBLANES) * SUBLANES
APOW_ROWS = CARRY_ROW0 + SUBLANES
ACOL_IM = LANES // 2
VMEM_BYTES = 64 * 1024 * 1024
VMEM_LIMIT = VMEM_BYTES - 8 * 1024 * 1024
VMEM_LIMIT_FFN = VMEM_BYTES - 4 * 1024 * 1024
VMEM_LIMIT_IN_PROJ = VMEM_BYTES - 2 * 1024 * 1024

BF16 = jnp.bfloat16
F32 = jnp.float32


def _rms(x, g):
    return x * lax.rsqrt(jnp.mean(x * x, axis=-1, keepdims=True) + EPS) * g


def _dot(a, b):
    return jnp.dot(a, b, preferred_element_type=F32)


def _ffn_prologue(x_ref, gin_ref, xn_ref, o_ref):
    x = x_ref[...]
    xn_ref[...] = _rms(x, gin_ref[...]).astype(BF16)
    o_ref[...] = x


def _ffn_accumulate(xn_ref, o_ref, wg, wu, wd):
    xn = xn_ref[...]
    gate = _dot(xn, wg)
    up = _dot(xn, wu)
    act = ((0.5 * gate) * jax.nn.sigmoid(gate) * up).astype(BF16)
    o_ref[...] += _dot(act, wd)


def _last_hidden_width(d_ff, tf):
    return d_ff - (pl.cdiv(d_ff, tf) - 1) * tf


def _cast_block_rows(nrows, steps):
    br = BF16_SUBLANES
    while nrows % br or nrows // br > steps:
        br += BF16_SUBLANES
    return br


def _ffn_stream_kernel(x_hbm, gin_ref, wg_ref, wu_ref, wd_ref, *refs, final_norm, has_head,
                       d_ff, n_cast):
    if final_norm:
        gout_ref, *refs = refs
    if has_head:
        head_hbm, *refs = refs
    cast_src = refs[:n_cast]
    o_hbm = refs[n_cast]
    cast_dst = refs[n_cast + 1:2 * n_cast + 1]
    xn_ref, acc_ref, in_sem, out_sem = refs[2 * n_cast + 1:]
    i = pl.program_id(0)
    f = pl.program_id(1)
    nt = pl.num_programs(0)
    nf = pl.num_programs(1)
    tm = xn_ref.shape[0]
    tf = wd_ref.shape[0]
    first = 1 if has_head else 0
    live = i >= first
    slot = i % 2
    other = 1 - slot
    acc = acc_ref.at[slot]

    def tile_rows(ref, tile):
        return ref.at[pl.ds(pl.multiple_of(tile * tm, tm), tm)]

    def x_copy(tile, s):
        return pltpu.make_async_copy(tile_rows(x_hbm, tile), acc_ref.at[s], in_sem.at[s])

    def o_copy(tile, s):
        return pltpu.make_async_copy(acc_ref.at[s], tile_rows(o_hbm, tile), out_sem.at[s])

    @pl.when(jnp.logical_and(i == 0, f == 0))
    def _():
        x_copy(first, first % 2).start()
        if has_head:
            head_copy = pltpu.make_async_copy(head_hbm, acc_ref.at[0], in_sem.at[0])
            head_copy.start()
            head_copy.wait()
            o_copy(0, 0).start()

    @pl.when(jnp.logical_and(live, f == 0))
    def _():
        x_copy(i, slot).wait()
        xn_ref[...] = _rms(acc[...], gin_ref[...]).astype(BF16)

    @pl.when(jnp.logical_and(live, f == 1))
    def _():
        @pl.when(i > 0)
        def _():
            o_copy(i - 1, other).wait()

        @pl.when(i + 1 < nt)
        def _():
            x_copy(i + 1, other).start()

    def accumulate(width):
        _ffn_accumulate(xn_ref, acc, wg_ref[:, :width], wu_ref[:, :width], wd_ref[:width, :])

    last_width = _last_hidden_width(d_ff, tf)
    pl.when(jnp.logical_and(live, f < nf - 1))(functools.partial(accumulate, tf))
    pl.when(jnp.logical_and(live, f == nf - 1))(functools.partial(accumulate, last_width))

    @pl.when(jnp.logical_and(live, f == nf - 1))
    def _():
        if final_norm:
            acc[...] = _rms(acc[...], gout_ref[...])
        o_copy(i, slot).start()

        @pl.when(i == nt - 1)
        def _():
            o_copy(i, slot).wait()

    for src, dst in zip(cast_src, cast_dst):
        dst[...] = src[...].astype(dst.dtype)


def _ffn_stream(x, g_in, wg, wu, wd, g_out=None, *, head=None, cast=(), tm=1024, tf):
    t, d = x.shape
    d_ff = wg.shape[1]
    final_norm = g_out is not None
    has_head = head is not None
    nt = t // tm
    nf = pl.cdiv(d_ff, tf)
    assert nf >= 2, "step 1 hands the free accumulator slot to the next tile's input"
    steps = nt * nf
    vec = pl.BlockSpec((1, d), lambda i, f: (0, 0))
    any_space = pl.BlockSpec(memory_space=pl.ANY)
    hidden = (lambda i, f: jnp.where(i == 0, 0, f)) if has_head else (lambda i, f: f)
    in_specs = [any_space, vec,
                pl.BlockSpec((d, tf), lambda i, f: (0, hidden(i, f))),
                pl.BlockSpec((d, tf), lambda i, f: (0, hidden(i, f))),
                pl.BlockSpec((tf, d), lambda i, f: (hidden(i, f), 0))]
    args = [x, g_in, wg, wu, wd]
    if final_norm:
        in_specs.append(vec)
        args.append(g_out)
    if has_head:
        in_specs.append(any_space)
        args.append(head)

    def cast_specs():
        specs = []
        for w in cast:
            nrows, ncols = w.shape
            br = _cast_block_rows(nrows, steps)
            specs.append(pl.BlockSpec(
                (br, ncols), lambda i, f, nb=nrows // br: (jnp.minimum(i * nf + f, nb - 1), 0)))
        return specs

    dma_sems = pltpu.SemaphoreType.DMA
    outs = pl.pallas_call(
        functools.partial(_ffn_stream_kernel, final_norm=final_norm, has_head=has_head,
                          d_ff=d_ff, n_cast=len(cast)),
        grid=(nt, nf),
        in_specs=in_specs + cast_specs(),
        out_specs=[any_space] + cast_specs(),
        out_shape=[jax.ShapeDtypeStruct((t, d), F32)]
        + [jax.ShapeDtypeStruct(w.shape, BF16) for w in cast],
        scratch_shapes=[pltpu.VMEM((tm, d), BF16), pltpu.VMEM((2, tm, d), F32),
                        dma_sems((2,)), dma_sems((2,))],
        compiler_params=pltpu.CompilerParams(
            dimension_semantics=("arbitrary", "arbitrary"),
            vmem_limit_bytes=VMEM_LIMIT_FFN),
        name="ffn_final" if final_norm else "ffn_hidden",
    )(*args, *cast)
    return outs if cast else outs[0]


def _ffn_head_kernel(x_ref, gin_ref, wg_ref, wu_ref, wd_ref, o_ref, wgb_ref, wub_ref, wdb_ref,
                     xn_ref, *, d_ff):
    f = pl.program_id(0)
    nf = pl.num_programs(0)
    tf = wd_ref.shape[0]

    @pl.when(f == 0)
    def _():
        _ffn_prologue(x_ref, gin_ref, xn_ref, o_ref)

    def accumulate(width):
        wgb_ref[:, :width] = wg_ref[:, :width].astype(BF16)
        wub_ref[:, :width] = wu_ref[:, :width].astype(BF16)
        wdb_ref[:width, :] = wd_ref[:width, :].astype(BF16)
        _ffn_accumulate(xn_ref, o_ref, wgb_ref[:, :width], wub_ref[:, :width], wdb_ref[:width, :])

    last_width = _last_hidden_width(d_ff, tf)
    pl.when(f < nf - 1)(functools.partial(accumulate, tf))
    pl.when(f == nf - 1)(functools.partial(accumulate, last_width))


def _ffn_head(x, g_in, wg, wu, wd, *, tm=1024, tf=256):
    d = x.shape[1]
    d_ff = wg.shape[1]
    const = lambda f: (0, 0)
    cols = pl.BlockSpec((d, tf), lambda f: (0, f))
    rows = pl.BlockSpec((tf, d), lambda f: (f, 0))
    return pl.pallas_call(
        functools.partial(_ffn_head_kernel, d_ff=d_ff),
        grid=(pl.cdiv(d_ff, tf),),
        in_specs=[pl.BlockSpec((tm, d), const, pipeline_mode=pl.Buffered(1)),
                  pl.BlockSpec((1, d), const), cols, cols, rows],
        out_specs=[pl.BlockSpec((tm, d), const), cols, cols, rows],
        out_shape=[jax.ShapeDtypeStruct((tm, d), F32),
                   jax.ShapeDtypeStruct(wg.shape, BF16),
                   jax.ShapeDtypeStruct(wu.shape, BF16),
                   jax.ShapeDtypeStruct(wd.shape, BF16)],
        scratch_shapes=[pltpu.VMEM((tm, d), BF16)],
        compiler_params=pltpu.CompilerParams(
            dimension_semantics=("arbitrary",), vmem_limit_bytes=VMEM_LIMIT_FFN),
        name="ffn_head",
    )(x, g_in, wg, wu, wd)


GATE_GROUP0 = 2


def _store_planes(ref, val):
    for p in range(ref.shape[0]):
        ref[p] = val[:, p * LANES:(p + 1) * LANES]


def _load_planes(ref, p0, p1):
    return jnp.concatenate([ref[p] for p in range(p0, p1)], axis=1)


def _in_proj_kernel(h_ref, g_ref, w_ref, cw_ref, cb_ref, v_ref, yb_ref, gate_ref,
                    xn_ref, planes_ref, tail_ref, *, cblk):
    s = pl.program_id(1)
    j = pl.program_id(2)
    ts = h_ref.shape[0]
    nchunk = ts // CHUNK

    @pl.when(j == 0)
    def _ssm_input_and_b_gate():
        xn_ref[...] = _rms(h_ref[...], g_ref[...]).astype(BF16)
        xn = xn_ref[...]
        _store_planes(planes_ref, _dot(xn, w_ref[:, :SSM_WIDTH]))
        for p in range(PLANES):
            for l in range(CHUNK):
                piece = planes_ref[p, pl.ds(l, nchunk, stride=CHUNK), :]
                v_ref[p, :, l * LANES:(l + 1) * LANES] = piece.astype(v_ref.dtype)
        _store_planes(planes_ref, _dot(xn, w_ref[:, SSM_WIDTH:]))

    @pl.when(jnp.logical_and(j == 1, s == 0))
    def _():
        tail_ref[...] = jnp.zeros_like(tail_ref)

    @pl.when(j == 1)
    def _conv():
        xn = xn_ref[...]
        row = lax.broadcasted_iota(jnp.int32, (ts, 1), 0)
        for c in range(CONV_WIDTH // cblk):
            cs = slice(c * cblk, (c + 1) * cblk)
            vs = slice(CONV_WIDTH + c * cblk, CONV_WIDTH + (c + 1) * cblk)
            z = _dot(xn, w_ref[:, cs]) * _dot(xn, w_ref[:, vs])
            prev1 = tail_ref[SUBLANES - 1:SUBLANES, cs]
            prev2 = tail_ref[SUBLANES - 2:SUBLANES - 1, cs]
            z1 = jnp.where(row == 0, prev1, pltpu.roll(z, 1, axis=0))
            z2 = jnp.where(row == 0, prev2, jnp.where(row == 1, prev1, pltpu.roll(z, 2, axis=0)))
            conv = cb_ref[:, cs] + cw_ref[0:1, cs] * z2 + cw_ref[1:2, cs] * z1 + cw_ref[2:3, cs] * z
            b_gate = _load_planes(planes_ref, c * cblk // LANES, (c + 1) * cblk // LANES)
            yb_ref[:, cs] = (b_gate * conv).astype(yb_ref.dtype)
            tail_ref[:, cs] = z[ts - SUBLANES:, :]

    @pl.when(j >= GATE_GROUP0)
    def _gates():
        gate_ref[...] = jax.nn.sigmoid(_dot(xn_ref[...], w_ref[...])).astype(gate_ref.dtype)


def _in_proj(h, g, w, conv_w, conv_b, *, batch, seq, ts=1024, cblk=256):
    t, d = h.shape
    nseq = seq // ts
    gw = 2 * CONV_WIDTH
    ngroups = w.shape[1] // gw
    tok = lambda b, s, j: (b * nseq + s, 0)
    const = lambda b, s, j: (0, 0)
    return pl.pallas_call(
        functools.partial(_in_proj_kernel, cblk=cblk),
        grid=(batch, nseq, ngroups),
        in_specs=[pl.BlockSpec((ts, d), tok),
                  pl.BlockSpec((1, d), const),
                  pl.BlockSpec((d, gw), lambda b, s, j: (0, j)),
                  pl.BlockSpec((CONV_K, CONV_WIDTH), const),
                  pl.BlockSpec((1, CONV_WIDTH), const)],
        out_specs=(pl.BlockSpec((PLANES, ts // CHUNK, CHUNK * LANES),
                                lambda b, s, j: (0, b * nseq + s, 0)),
                   pl.BlockSpec((ts, CONV_WIDTH), tok),
                   pl.BlockSpec((ts, gw),
                                lambda b, s, j: (b * nseq + s, jnp.maximum(j - GATE_GROUP0, 0)))),
        out_shape=(jax.ShapeDtypeStruct((PLANES, t // CHUNK, CHUNK * LANES), BF16),
                   jax.ShapeDtypeStruct((t, CONV_WIDTH), BF16),
                   jax.ShapeDtypeStruct((t, 2 * d), BF16)),
        scratch_shapes=[pltpu.VMEM((ts, d), BF16),
                        pltpu.VMEM((PLANES, ts, LANES), F32),
                        pltpu.VMEM((SUBLANES, CONV_WIDTH), F32)],
        compiler_params=pltpu.CompilerParams(
            dimension_semantics=("arbitrary", "arbitrary", "arbitrary"),
            vmem_limit_bytes=VMEM_LIMIT_IN_PROJ),
        name="in_proj",
    )(h, g, w, conv_w, conv_b)


def _block_diag_tile(src, period, rows_per_group, cols_per_group):
    nrows = src.shape[0]
    ncols = cols_per_group * GROUPS_PER_PLANE
    k = lax.broadcasted_iota(jnp.int32, (src.shape[1], ncols), 0)
    c = lax.broadcasted_iota(jnp.int32, (src.shape[1], ncols), 1)
    pick = (k == c % period).astype(BF16)
    hi = src.astype(BF16)
    rest = src - hi.astype(F32)
    mid = rest.astype(BF16)
    lo = (rest - mid.astype(F32)).astype(BF16)
    tiled = (_dot(hi, pick) + _dot(mid, pick)) + _dot(lo, pick)
    r = lax.broadcasted_iota(jnp.int32, (nrows, ncols), 0)
    c = lax.broadcasted_iota(jnp.int32, (nrows, ncols), 1)
    return jnp.where(r // rows_per_group == c // cols_per_group, tiled, 0.0)


def _ssm_kernel(x_ref, btr_ref, bti_ref, ctr_ref, cti_ref, arow_ref, acol_ref, d_ref, o_ref,
                toep_ref, win_ref, wout_ref, y_ref, *, chunks):
    half = HALF_STATE

    @pl.when(pl.program_id(1) == 0)
    def _build_operators():
        btr = _block_diag_tile(btr_ref[...], SSM_STATE, SSM_GROUP, SSM_STATE)
        bti = _block_diag_tile(bti_ref[...], SSM_STATE, SSM_GROUP, SSM_STATE)
        ctr = _block_diag_tile(ctr_ref[...], SSM_GROUP, SSM_STATE, SSM_GROUP)
        cti = _block_diag_tile(cti_ref[...], SSM_GROUP, SSM_STATE, SSM_GROUP)
        c_cat = jnp.concatenate([ctr, -cti], axis=0).astype(BF16)
        zero = jnp.zeros((LANES, LANES), BF16)
        for k in range(CHUNK):
            ar = arow_ref[k:k + 1, :half]
            ai = arow_ref[k:k + 1, half:]
            wr = btr * ar - bti * ai
            wi = btr * ai + bti * ar
            rows = slice((CHUNK - 1 - k) * LANES, (CHUNK - k) * LANES)
            w_cat = jnp.concatenate([wr, wi], axis=1).astype(BF16)
            win_ref[rows, :] = w_cat
            lag = _dot(w_cat, c_cat).astype(BF16)
            for lin in range(CHUNK - k):
                lout = lin + k
                toep_ref[lin * LANES:(lin + 1) * LANES, lout * LANES:(lout + 1) * LANES] = lag
        for lin in range(1, CHUNK, MXU_DIM // LANES):
            toep_ref[lin * LANES:(lin + 1) * LANES, (lin - 1) * LANES:lin * LANES] = zero
        for l in range(CHUNK):
            arc = acol_ref[:, l + 1:l + 2]
            aic = acol_ref[:, ACOL_IM + l + 1:ACOL_IM + l + 2]
            cols = slice(l * LANES, (l + 1) * LANES)
            wout_ref[:half, cols] = (ctr * arc - cti * aic).astype(BF16)
            wout_ref[half:, cols] = (-(ctr * aic + cti * arc)).astype(BF16)

    rows, width = x_ref.shape
    s_in = _dot(x_ref[...], win_ref[...])
    sr = s_in[:, :half]
    si = s_in[:, half:]
    for cb in range(width // MXU_DIM):
        cols = slice(cb * MXU_DIM, (cb + 1) * MXU_DIM)
        kk = (cb + 1) * MXU_DIM
        y_ref[:, cols] = (_dot(x_ref[:, :kk], toep_ref[:kk, cols])
                          + d_ref[:, cols] * x_ref[:, cols].astype(F32))
    cidx = lax.broadcasted_iota(jnp.int32, (rows, 1), 0) % chunks
    for k in range(SCAN_STEPS_IN_GROUP):
        sh = 1 << k
        ar = arow_ref[SCAN_ROW0 + k:SCAN_ROW0 + k + 1, :half]
        ai = arow_ref[SCAN_ROW0 + k:SCAN_ROW0 + k + 1, half:]
        keep = cidx % SUBLANES >= sh
        pr = jnp.where(keep, pltpu.roll(sr, sh, axis=0), 0.0)
        pi = jnp.where(keep, pltpu.roll(si, sh, axis=0), 0.0)
        sr, si = sr + (ar * pr - ai * pi), si + (ar * pi + ai * pr)
    cr = arow_ref[CARRY_ROW0:CARRY_ROW0 + SUBLANES, :half]
    ci = arow_ref[CARRY_ROW0:CARRY_ROW0 + SUBLANES, half:]
    out_r, out_i = [], []
    for g in range(rows // SUBLANES):
        gr = sr[g * SUBLANES:(g + 1) * SUBLANES, :]
        gi = si[g * SUBLANES:(g + 1) * SUBLANES, :]
        if g % (chunks // SUBLANES):
            lr = jnp.broadcast_to(out_r[-1][SUBLANES - 1:, :], gr.shape)
            li = jnp.broadcast_to(out_i[-1][SUBLANES - 1:, :], gi.shape)
            gr, gi = gr + (cr * lr - ci * li), gi + (cr * li + ci * lr)
        out_r.append(gr)
        out_i.append(gi)
    sr = jnp.concatenate(out_r, axis=0)
    si = jnp.concatenate(out_i, axis=0)
    first = cidx >= 1
    pr = jnp.where(first, pltpu.roll(sr, 1, axis=0), 0.0).astype(BF16)
    pi = jnp.where(first, pltpu.roll(si, 1, axis=0), 0.0).astype(BF16)
    s_prev = jnp.concatenate([pr, pi], axis=1)
    for cb in range(width // MXU_DIM):
        cols = slice(cb * MXU_DIM, (cb + 1) * MXU_DIM)
        y = y_ref[:, cols] + _dot(s_prev, wout_ref[:, cols])
        o_ref[:, cols] = jax.nn.gelu(y).astype(o_ref.dtype)


def _ssm(xc, btr, bti, ctr, cti, arow, acol, d_tiled, *, chunks, rows=1024):
    planes, nrows, width = xc.shape
    nh = nrows // rows

    def plane(*shape):
        return pl.BlockSpec((None,) + shape, lambda j, h: (j,) + (0,) * len(shape))

    return pl.pallas_call(
        functools.partial(_ssm_kernel, chunks=chunks),
        grid=(planes, nh),
        in_specs=[
            pl.BlockSpec((None, rows, width), lambda j, h: (j, h, 0)),
            plane(LANES, LANES), plane(LANES, LANES),
            plane(HALF_STATE, LANES), plane(HALF_STATE, LANES),
            plane(APOW_ROWS, PLANE_STATE), plane(HALF_STATE, LANES),
            plane(1, width),
        ],
        out_specs=pl.BlockSpec((None, rows, width), lambda j, h: (j, h, 0)),
        out_shape=jax.ShapeDtypeStruct((planes, nrows, width), BF16),
        scratch_shapes=[pltpu.VMEM((width, width), BF16),
                        pltpu.VMEM((width, PLANE_STATE), BF16),
                        pltpu.VMEM((PLANE_STATE, width), BF16),
                        pltpu.VMEM((rows, width), F32)],
        compiler_params=pltpu.CompilerParams(
            dimension_semantics=("arbitrary", "arbitrary"), vmem_limit_bytes=VMEM_LIMIT),
        name="ssm_scan",
    )(xc, btr, bti, ctr, cti, arow, acol, d_tiled)


def _ssm_tables(lam_re, lam_im, log_dt, b_re, b_im, c_re, c_im, d_skip):
    lam_re = jnp.minimum(lam_re, -1e-4)
    dt = jnp.exp(log_dt)[:, None]
    mag = jnp.exp(lam_re * dt)
    a_re = mag * jnp.cos(lam_im * dt)
    a_im = mag * jnp.sin(lam_im * dt)
    den = lam_re * lam_re + lam_im * lam_im
    p = a_re - 1.0
    f_re = ((p * lam_re + a_im * lam_im) / den)[:, :, None]
    f_im = ((a_im * lam_re - p * lam_im) / den)[:, :, None]
    bb_re = f_re * b_re - f_im * b_im
    bb_im = f_re * b_im + f_im * b_re

    N, C, P, L = SSM_STATE, SSM_GROUP, PLANES, CHUNK

    def bt(bb):
        m = jnp.transpose(bb, (0, 2, 1)).reshape(P, LANES, N)
        return jnp.pad(m, ((0, 0), (0, 0), (0, LANES - N)))

    def ct(c):
        m = jnp.transpose(c, (0, 2, 1)).reshape(P, HALF_STATE, C)
        return jnp.pad(m, ((0, 0), (0, 0), (0, LANES - C)))

    ks = (list(range(L + 1)) + [L << m for m in range(SCAN_STEPS_IN_GROUP)])
    ks += [0] * (CARRY_ROW0 - len(ks)) + [L * (r + 1) for r in range(SUBLANES)]
    ks = jnp.asarray(ks, F32).reshape(-1, 1, 1)
    m = jnp.exp(ks * (lam_re * dt))
    pr = m * jnp.cos(ks * (lam_im * dt))
    pi = m * jnp.sin(ks * (lam_im * dt))
    arow = jnp.concatenate([pr.reshape(APOW_ROWS, P, HALF_STATE),
                            pi.reshape(APOW_ROWS, P, HALF_STATE)], axis=2)
    arow = jnp.transpose(arow, (1, 0, 2))

    def col(pw):
        c = jnp.transpose(pw[:L + 1].reshape(L + 1, P, HALF_STATE), (1, 2, 0))
        return jnp.pad(c, ((0, 0), (0, 0), (0, ACOL_IM - (L + 1))))

    acol = jnp.concatenate([col(pr), col(pi)], axis=2)
    d_tiled = jnp.tile(d_skip.reshape(P, 1, LANES), (1, 1, L))
    return bt(bb_re), bt(bb_im), ct(c_re), ct(c_im), arow, acol, d_tiled


def _mix_out_kernel(g_ref, yb_ref, gate_ref, h_ref, wglu_ref, bglu_ref, wa_ref, wb_ref, wo_ref, o_ref,
                    gs_ref):
    nchunk = g_ref.shape[1]
    for p in range(PLANES):
        for l in range(CHUNK):
            piece = g_ref[p, :, l * LANES:(l + 1) * LANES].astype(F32)
            gs_ref[p, pl.ds(l, nchunk, stride=CHUNK), :] = piece
    g = _load_planes(gs_ref, 0, PLANES)
    glu = _dot(g.astype(BF16), wglu_ref[...]) + bglu_ref[...]
    y_a = (g * jax.nn.sigmoid(glu)).astype(BF16)
    z_a = _dot(y_a, wa_ref[...])
    z_b = _dot(yb_ref[...], wb_ref[...])
    merged = (gate_ref[:, :D_MODEL].astype(F32) * z_a
              + gate_ref[:, D_MODEL:].astype(F32) * z_b).astype(BF16)
    o_ref[...] = h_ref[...] + _dot(merged, wo_ref[...])


def _mix_out(g_planes, y_b, gates, h, w_glu, b_glu, w_a, w_b, w_o, *, tm=512):
    t, d = h.shape

    def const(shape):
        return pl.BlockSpec(shape, lambda i: (0,) * len(shape), pipeline_mode=pl.Buffered(1))

    return pl.pallas_call(
        _mix_out_kernel,
        grid=(t // tm,),
        in_specs=[
            pl.BlockSpec((PLANES, tm // CHUNK, CHUNK * LANES), lambda i: (0, i, 0)),
            pl.BlockSpec((tm, CONV_WIDTH), lambda i: (i, 0)),
            pl.BlockSpec((tm, 2 * d), lambda i: (i, 0)),
            pl.BlockSpec((tm, d), lambda i: (i, 0)),
            const((SSM_WIDTH, SSM_WIDTH)), const((1, SSM_WIDTH)),
            const((SSM_WIDTH, d)), const((CONV_WIDTH, d)), const((d, d)),
        ],
        out_specs=pl.BlockSpec((tm, d), lambda i: (i, 0)),
        out_shape=jax.ShapeDtypeStruct((t, d), F32),
        scratch_shapes=[pltpu.VMEM((PLANES, tm, LANES), F32)],
        compiler_params=pltpu.CompilerParams(
            dimension_semantics=("parallel",), vmem_limit_bytes=VMEM_LIMIT),
        name="mix_out",
    )(g_planes, y_b, gates, h, w_glu, b_glu, w_a, w_b, w_o)


def kernel(x, ffn1_norm, ffn1_w_gate, ffn1_w_up, ffn1_w_down, mix_norm, w_in, ssm_lambda_re, ssm_lambda_im, ssm_log_dt, ssm_b_re, ssm_b_im, ssm_c_re, ssm_c_im, ssm_d, ssm_w_glu, ssm_b_glu, ssm_w_out, conv_w, conv_b, conv_w_out, w_o, ffn2_norm, ffn2_w_gate, ffn2_w_up, ffn2_w_down, final_norm):
    batch, seq, d = x.shape
    t = batch * seq
    chunks = seq // CHUNK

    vec = lambda g: g.reshape(1, -1).astype(F32)

    later = (ffn2_w_gate, ffn2_w_up, ffn2_w_down, w_in, ssm_w_glu, ssm_w_out, conv_w_out, w_o)
    x2 = x.reshape(t, d)
    h1_head, wg1, wu1, wd1 = _ffn_head(x2, vec(ffn1_norm), ffn1_w_gate, ffn1_w_up, ffn1_w_down)
    h1, wg2, wu2, wd2, w_in_b, w_glu_b, w_a_b, w_b_b, w_o_b = _ffn_stream(
        x2, vec(ffn1_norm), wg1, wu1, wd1, head=h1_head, cast=later, tf=512)

    v_chunks, y_b, gates = _in_proj(h1, vec(mix_norm), w_in_b, conv_w.astype(F32), vec(conv_b),
                                    batch=batch, seq=seq)

    tables = _ssm_tables(ssm_lambda_re, ssm_lambda_im, ssm_log_dt, ssm_b_re, ssm_b_im,
                         ssm_c_re, ssm_c_im, ssm_d)
    g_planes = _ssm(v_chunks, *tables, chunks=chunks)

    h2 = _mix_out(g_planes, y_b, gates, h1, w_glu_b, vec(ssm_b_glu), w_a_b, w_b_b, w_o_b)

    out = _ffn_stream(h2, vec(ffn2_norm), wg2, wu2, wd2, vec(final_norm), tf=1024)
    return out.reshape(batch, seq, d)
```
